```python
import jax, jax.numpy as jnp
from jax import lax
import numpy as np

D_MODEL = 2048
BATCH = 16
SEQ = 256
DEPTH = 2
DEC_BATCH = 8
DEC_SEQ = 4096
PAST_LEN = 256

GRID_W = 64
D_CONV = 512
CONV_WIDTH = 31
D_RG = 512
RG_BLOCKS = 8
RG_BLOCK = D_RG // RG_BLOCKS
RG_CONV_WIDTH = 4
RG_C = 8.0
N_RET_HEADS = 8
RET_HEAD_DIM = 128
D_RET = N_RET_HEADS * RET_HEAD_DIM
RET_CHUNK = 128
ROPE_BASE = 10000.0
D_MIX = D_CONV + D_RG + D_RET
D_IN = 2 * D_CONV + 2 * D_RG + 4 * D_RET
N_EXPERTS = 16
EC_FACTOR = 2
D_EXPERT = 5632
N_MOD = 6
EPS = 1e-6

kernel_name = "hybrid_conv_rglru_retention_ec_diffusion_step"


def _rmsnorm(x, g):
    xf = x.astype(jnp.float32)
    y = xf * lax.rsqrt(jnp.mean(xf * xf, axis=-1, keepdims=True) + EPS)
    return (y * g.astype(jnp.float32)).astype(x.dtype)


def _layernorm(x, g, b):
    xf = x.astype(jnp.float32)
    mu = jnp.mean(xf, axis=-1, keepdims=True)
    var = jnp.mean(jnp.square(xf - mu), axis=-1, keepdims=True)
    y = (xf - mu) * lax.rsqrt(var + EPS) * g.astype(jnp.float32) + b.astype(jnp.float32)
    return y.astype(x.dtype)


def _dwconv(x, w, b, pad_left):
    width, ch = w.shape
    y = lax.conv_general_dilated(
        x, w[:, None, :].astype(x.dtype), window_strides=(1,),
        padding=[(pad_left, width - 1 - pad_left)],
        dimension_numbers=('NWC', 'WIO', 'NWC'), feature_group_count=ch)
    return y + b.astype(x.dtype)


def _linear_combine(e1, e2):
    a1, b1 = e1
    a2, b2 = e2
    return a1 * a2, a2 * b1 + b2


def _rglru_scan(x, w_a, b_a, w_x, b_x, lam, h0):
    bsz, length, width = x.shape
    xb = x.reshape(bsz, length, RG_BLOCKS, RG_BLOCK)
    r = jax.nn.sigmoid(jnp.einsum('blgi,gij->blgj', xb, w_a.astype(jnp.float32)).reshape(bsz, length, width)
                       + b_a.astype(jnp.float32))
    i = jax.nn.sigmoid(jnp.einsum('blgi,gij->blgj', xb, w_x.astype(jnp.float32)).reshape(bsz, length, width)
                       + b_x.astype(jnp.float32))
    log_a = -RG_C * r * jax.nn.softplus(-lam.astype(jnp.float32))
    a = jnp.exp(log_a)
    u = jnp.sqrt(-jnp.expm1(2.0 * log_a)) * (i * x)
    u = u.at[:, 0].add(a[:, 0] * h0.astype(jnp.float32))
    _, h = lax.associative_scan(_linear_combine, (a, u), axis=1)
    return h, h[:, -1]


def _rglru_bidir(x, w_a, b_a, w_x, b_x, lam, h0):
    hf, sf = _rglru_scan(x, w_a[0], b_a[0], w_x[0], b_x[0], lam[0], h0[:, 0])
    hb, sb = _rglru_scan(x[:, ::-1], w_a[1], b_a[1], w_x[1], b_x[1], lam[1], h0[:, 1])
    return hf + hb[:, ::-1], jnp.stack([sf, sb], axis=1)


def _retention_scan(q, k, v, log_g, s0):
    bsz, length, nh, dk = q.shape
    dv = v.shape[-1]
    n = length // RET_CHUNK
    qc = q.reshape(bsz, n, RET_CHUNK, nh, dk)
    kc = k.reshape(bsz, n, RET_CHUNK, nh, dk)
    vc = v.reshape(bsz, n, RET_CHUNK, nh, dv)
    idx = jnp.arange(RET_CHUNK, dtype=jnp.float32)
    rel = idx[:, None] - idx[None, :]
    decay = jnp.where(rel[None] >= 0, jnp.exp(jnp.maximum(rel, 0.0)[None] * log_g[:, None, None]), 0.0)
    scores = jnp.einsum('bnihd,bnjhd->bnhij', qc, kc) * decay
    inner = jnp.einsum('bnhij,bnjhe->bnihe', scores, vc)
    k_w = jnp.exp((RET_CHUNK - 1.0 - idx)[:, None] * log_g[None, :])
    kv = jnp.einsum('bnjhd,jh,bnjhe->bnhde', kc, k_w, vc)
    chunk_decay = jnp.exp(RET_CHUNK * log_g)[None, :, None, None]

    def step(s, kv_c):
        return chunk_decay * s + kv_c, s

    s_final, s_prev = lax.scan(step, s0.astype(jnp.float32), jnp.moveaxis(kv, 1, 0))
    s_prev = jnp.moveaxis(s_prev, 0, 1)
    q_w = jnp.exp((idx + 1.0)[:, None] * log_g[None, :])
    cross = jnp.einsum('bnihd,ih,bnhde->bnihe', qc, q_w, s_prev)
    return (inner + cross).reshape(bsz, length, nh, dv), s_final


def _retention_bidir(q, k, v, decay_logit, s0):
    log_g = jax.nn.log_sigmoid(decay_logit.astype(jnp.float32))
    of, sf = _retention_scan(q, k, v, log_g[0], s0[:, 0])
    ob, sb = _retention_scan(q[:, ::-1], k[:, ::-1], v[:, ::-1], log_g[1], s0[:, 1])
    return of + ob[:, ::-1], jnp.stack([sf, sb], axis=1)


def _rope_axis(x, pos):
    half = x.shape[-1] // 2
    freq = ROPE_BASE ** (-jnp.arange(half, dtype=jnp.float32) / half)
    ang = pos.astype(jnp.float32)[:, None] * freq[None, :]
    cos = jnp.cos(ang)[None, :, None, :]
    sin = jnp.sin(ang)[None, :, None, :]
    x1, x2 = x[..., :half], x[..., half:]
    return jnp.concatenate([x1 * cos - x2 * sin, x1 * sin + x2 * cos], axis=-1)


def _rotary_2d(x, rows, cols):
    half = x.shape[-1] // 2
    return jnp.concatenate([_rope_axis(x[..., :half], rows), _rope_axis(x[..., half:], cols)], axis=-1)


def _mixer(h, p, pos, rg_h0, ret_s0):
    bsz, length, _ = h.shape
    proj = jnp.einsum('bld,de->ble', h, p['w_in'])
    widths = [D_CONV, D_CONV, D_RG, D_RG, D_RET, D_RET, D_RET]
    cuts = []
    acc = 0
    for w in widths:
        acc += w
        cuts.append(acc)
    ca, cb, rg_gate, rg_x, q, k, v, g = jnp.split(proj, cuts, axis=-1)

    u = ca * jax.nn.sigmoid(cb)
    u = _dwconv(u, p['conv_w'], p['conv_b'], CONV_WIDTH // 2)
    u = jax.nn.silu(_layernorm(u, p['conv_ln_g'], p['conv_ln_b']))

    xr = _dwconv(rg_x, p['rg_conv_w'], p['rg_conv_b'], RG_CONV_WIDTH // 2).astype(jnp.float32)
    hr, rg_state = _rglru_bidir(xr, p['rg_w_a'], p['rg_b_a'], p['rg_w_x'], p['rg_b_x'], p['rg_lambda'], rg_h0)
    r_out = (jax.nn.gelu(rg_gate.astype(jnp.float32)) * hr).astype(h.dtype)

    qh = q.reshape(bsz, length, N_RET_HEADS, RET_HEAD_DIM).astype(jnp.float32)
    kh = k.reshape(bsz, length, N_RET_HEADS, RET_HEAD_DIM).astype(jnp.float32) * (RET_HEAD_DIM ** -0.5)
    vh = v.reshape(bsz, length, N_RET_HEADS, RET_HEAD_DIM).astype(jnp.float32)
    if pos is not None:
        qh = _rotary_2d(qh, pos[0], pos[1])
        kh = _rotary_2d(kh, pos[0], pos[1])
    o, ret_state = _retention_bidir(qh, kh, vh, p['ret_decay'], ret_s0)
    mu = jnp.mean(o, axis=-1, keepdims=True)
    var = jnp.mean(jnp.square(o - mu), axis=-1, keepdims=True)
    o = ((o - mu) * lax.rsqrt(var + EPS)).reshape(bsz, length, D_RET) * p['ret_gn_g'].astype(jnp.float32)
    o = (jax.nn.silu(g.astype(jnp.float32)) * o).astype(h.dtype)

    y = jnp.einsum('blm,md->bld', jnp.concatenate([u, r_out, o], axis=-1), p['w_out'])
    return y, rg_state, ret_state


def _expert_choice(h, router_w, w_gate, w_up, w_down):
    bsz, length, _ = h.shape
    cap = EC_FACTOR * length // N_EXPERTS
    logits = jnp.einsum('bld,de->ble', h, router_w).astype(jnp.float32)
    aff = jax.nn.softmax(logits, axis=-1)
    gate, idx = lax.top_k(jnp.swapaxes(aff, 1, 2), cap)
    idx_t = jnp.swapaxes(idx, 0, 1)
    gate_t = jnp.swapaxes(gate, 0, 1).astype(h.dtype)
    b_ar = jnp.arange(bsz)[:, None]

    def expert(args):
        idx_e, g_e, wg, wu, wd = args
        xe = h[b_ar, idx_e]
        ye = jnp.einsum('bcf,fd->bcd', jax.nn.silu(jnp.einsum('bcd,df->bcf', xe, wg))
                        * jnp.einsum('bcd,df->bcf', xe, wu), wd)
        return ye * g_e[..., None]

    ys = lax.map(expert, (idx_t, gate_t, w_gate, w_up, w_down))
    return jnp.zeros_like(h).at[b_ar[None], idx_t].add(ys)


def _layer(x, cond, p, pos, rg_h0, ret_s0):
    mod = jnp.einsum('bd,de->be', jax.nn.silu(cond), p['w_mod']) + p['b_mod']
    sh1, sc1, g1, sh2, sc2, g2 = jnp.split(mod[:, None, :], N_MOD, axis=-1)
    h = _rmsnorm(x, p['norm1_g']) * (1.0 + sc1) + sh1
    y, rg_s, ret_s = _mixer(h, p, pos, rg_h0, ret_s0)
    x = x + g1 * y
    h = _rmsnorm(x, p['norm2_g']) * (1.0 + sc2) + sh2
    x = x + g2 * _expert_choice(h, p['router_w'], p['exp_w_gate'], p['exp_w_up'], p['exp_w_down'])
    return x, rg_s, ret_s


def setup_inputs(seed: int = 0) -> dict:
    key = jax.random.key(seed)
    ks = iter(jax.random.split(key, 40))
    f32 = jnp.float32

    def nrm(shape, scale):
        return jax.random.normal(next(ks), shape, f32) * scale

    u = jax.random.uniform(next(ks), (DEPTH, 2, D_RG), f32, minval=0.9, maxval=0.999)
    a_base = u ** (1.0 / RG_C)
    rg_lambda = jnp.log(a_base) - jnp.log1p(-a_base)
    gamma = 1.0 - 2.0 ** (-5.0 - jnp.arange(N_RET_HEADS, dtype=f32))
    ret_decay = (jnp.log(gamma) - jnp.log1p(-gamma))[None, None, :] + nrm((DEPTH, 2, N_RET_HEADS), 0.05)
    return {
        'x_prompt': nrm((BATCH, SEQ, D_MODEL), 1.0),
        'x_sample': nrm((DEC_BATCH, DEC_SEQ, D_MODEL), 1.0),
        'state_rglru': nrm((DEC_BATCH, DEPTH, 2, D_RG), 0.5),
        'state_ret': nrm((DEC_BATCH, DEPTH, 2, N_RET_HEADS, RET_HEAD_DIM, RET_HEAD_DIM), 0.5),
        'c': nrm((DEC_BATCH, D_MODEL), 1.0),
        'c_ctx': nrm((D_MODEL,), 1.0),
        'norm1_g': 1.0 + nrm((DEPTH, D_MODEL), 0.02),
        'norm2_g': 1.0 + nrm((DEPTH, D_MODEL), 0.02),
        'w_mod': nrm((DEPTH, D_MODEL, N_MOD * D_MODEL), 0.5 * D_MODEL ** -0.5),
        'b_mod': nrm((DEPTH, N_MOD * D_MODEL), 0.02),
        'w_in': nrm((DEPTH, D_MODEL, D_IN), D_MODEL ** -0.5),
        'conv_w': nrm((DEPTH, CONV_WIDTH, D_CONV), CONV_WIDTH ** -0.5),
        'conv_b': nrm((DEPTH, D_CONV), 0.02),
        'conv_ln_g': 1.0 + nrm((DEPTH, D_CONV), 0.02),
        'conv_ln_b': nrm((DEPTH, D_CONV), 0.02),
        'rg_conv_w': nrm((DEPTH, RG_CONV_WIDTH, D_RG), RG_CONV_WIDTH ** -0.5),
        'rg_conv_b': nrm((DEPTH, D_RG), 0.02),
        'rg_w_a': nrm((DEPTH, 2, RG_BLOCKS, RG_BLOCK, RG_BLOCK), RG_BLOCK ** -0.5),
        'rg_b_a': nrm((DEPTH, 2, D_RG), 0.02),
        'rg_w_x': nrm((DEPTH, 2, RG_BLOCKS, RG_BLOCK, RG_BLOCK), RG_BLOCK ** -0.5),
        'rg_b_x': nrm((DEPTH, 2, D_RG), 0.02),
        'rg_lambda': rg_lambda,
        'ret_decay': ret_decay,
        'ret_gn_g': 1.0 + nrm((DEPTH, D_RET), 0.02),
        'w_out': nrm((DEPTH, D_MIX, D_MODEL), D_MIX ** -0.5),
        'router_w': nrm((DEPTH, D_MODEL, N_EXPERTS), D_MODEL ** -0.5),
        'exp_w_gate': nrm((DEPTH, N_EXPERTS, D_MODEL, D_EXPERT), D_MODEL ** -0.5),
        'exp_w_up': nrm((DEPTH, N_EXPERTS, D_MODEL, D_EXPERT), D_MODEL ** -0.5),
        'exp_w_down': nrm((DEPTH, N_EXPERTS, D_EXPERT, D_MODEL), D_EXPERT ** -0.5),
        'final_g': 1.0 + nrm((D_MODEL,), 0.02),
    }


def reference(x_prompt, x_sample, state_rglru, state_ret, c, c_ctx, norm1_g, norm2_g, w_mod, b_mod, w_in,
              conv_w, conv_b, conv_ln_g, conv_ln_b, rg_conv_w, rg_conv_b, rg_w_a, rg_b_a, rg_w_x, rg_b_x,
              rg_lambda, ret_decay, ret_gn_g, w_out, router_w, exp_w_gate, exp_w_up, exp_w_down, final_g):
    def layer_params(l):
        return dict(norm1_g=norm1_g[l], norm2_g=norm2_g[l], w_mod=w_mod[l], b_mod=b_mod[l], w_in=w_in[l],
                    conv_w=conv_w[l], conv_b=conv_b[l], conv_ln_g=conv_ln_g[l], conv_ln_b=conv_ln_b[l],
                    rg_conv_w=rg_conv_w[l], rg_conv_b=rg_conv_b[l], rg_w_a=rg_w_a[l], rg_b_a=rg_b_a[l],
                    rg_w_x=rg_w_x[l], rg_b_x=rg_b_x[l], rg_lambda=rg_lambda[l], ret_decay=ret_decay[l],
                    ret_gn_g=ret_gn_g[l], w_out=w_out[l], router_w=router_w[l], exp_w_gate=exp_w_gate[l],
                    exp_w_up=exp_w_up[l], exp_w_down=exp_w_down[l])

    bp = x_prompt.shape[0]
    zero_rg = jnp.zeros((bp, 2, D_RG), jnp.float32)
    zero_ret = jnp.zeros((bp, 2, N_RET_HEADS, RET_HEAD_DIM, RET_HEAD_DIM), jnp.float32)
    x = x_prompt
    rg_states = []
    ret_states = []
    for l in range(DEPTH):
        x, rg_s, ret_s = _layer(x, c_ctx[None, :], layer_params(l), None, zero_rg, zero_ret)
        rg_states.append(rg_s)
        ret_states.append(ret_s)
    y_prompt = _rmsnorm(x, final_g)
    new_state_rglru = jnp.stack(rg_states, axis=1)
    new_state_ret = jnp.stack(ret_states, axis=1)

    n_lat = x_sample.shape[1]
    rows = n_lat // GRID_W
    pos = (jnp.repeat(jnp.arange(rows), GRID_W), jnp.tile(jnp.arange(GRID_W), rows))
    z = x_sample
    for l in range(DEPTH):
        z, _, _ = _layer(z, c, layer_params(l), pos, state_rglru[:, l], state_ret[:, l])
    y_sample = _rmsnorm(z, final_g)
    return (y_prompt, y_sample, new_state_rglru, new_state_ret)
```

```python
import functools

import jax
import jax.numpy as jnp
from jax import lax
from jax.experimental import pallas as pl
from jax.experimental.pallas import tpu as pltpu

F32 = jnp.float32
BF16 = jnp.bfloat16

D_MODEL = 2048
D_CONV = 512
CONV_WIDTH = 31
D_RG = 512
RG_BLOCKS = 8
RG_CONV_WIDTH = 4
RG_C = 8.0
N_RET_HEADS = 8
RET_HEAD_DIM = 128
D_RET = N_RET_HEADS * RET_HEAD_DIM
RET_CHUNK = 128
ROPE_BASE = 10000.0
D_IN = 2 * D_CONV + 2 * D_RG + 4 * D_RET
N_EXPERTS = 16
EC_FACTOR = 2
N_MOD = 6
EPS = 1e-6
GRID_W = 64

V7X_VMEM_LIMIT_BYTES = 56 * 1024 * 1024
SEQ_TILE = 256
MOD_ROWS = 16


def _params(sem, vmem=V7X_VMEM_LIMIT_BYTES):
    return pltpu.CompilerParams(dimension_semantics=sem, vmem_limit_bytes=vmem)


def _silu(x):
    return x * jax.nn.sigmoid(x)


def _mod_kernel(c_ref, w_ref, b_ref, o_ref):
    c = _silu(c_ref[...]).astype(BF16)
    o_ref[0] = jnp.dot(c, w_ref[0].astype(BF16), preferred_element_type=F32) + b_ref[0]


def _modulation(cond, w_mod, b_mod):
    depth, d, n = w_mod.shape
    tn = 1024
    return pl.pallas_call(
        _mod_kernel,
        out_shape=jax.ShapeDtypeStruct((depth, MOD_ROWS, n), F32),
        grid=(depth, n // tn),
        in_specs=[
            pl.BlockSpec((MOD_ROWS, d), lambda l, j: (0, 0)),
            pl.BlockSpec((1, d, tn), lambda l, j: (l, 0, j)),
            pl.BlockSpec((1, 1, tn), lambda l, j: (l, 0, j)),
        ],
        out_specs=pl.BlockSpec((1, MOD_ROWS, tn), lambda l, j: (l, 0, j)),
        compiler_params=_params(("parallel", "parallel")),
        name="modulation",
    )(cond, w_mod, b_mod.reshape(depth, 1, n))


def _mod_row(m_ctx, seq_rows, tm):
    def f(i):
        start = i * tm
        return jnp.where(start < m_ctx, 0, 1 + (start - m_ctx) // seq_rows)
    return f


def _in_proj_kernel(x_ref, g_ref, sc_ref, sh_ref, w_ref, o_ref, h_ref):
    @pl.when(pl.program_id(1) == 0)
    def _():
        x = x_ref[...]
        y = x * lax.rsqrt(jnp.mean(x * x, axis=-1, keepdims=True) + EPS) * g_ref[...]
        h_ref[...] = (y * (1.0 + sc_ref[0]) + sh_ref[0]).astype(BF16)

    o_ref[...] = jnp.dot(h_ref[...], w_ref[...], preferred_element_type=F32)


def _in_proj(x, norm_g, mod3, layer, w_in_bf, m_ctx, lat_len):
    m, d = x.shape
    n = w_in_bf.shape[-1]
    tm, tn = 1024, 1024
    row = _mod_row(m_ctx, lat_len, tm)
    base = layer * MOD_ROWS * N_MOD
    return pl.pallas_call(
        _in_proj_kernel,
        out_shape=jax.ShapeDtypeStruct((m, n), F32),
        grid=(m // tm, n // tn),
        in_specs=[
            pl.BlockSpec((tm, d), lambda i, j: (i, 0)),
            pl.BlockSpec((1, d), lambda i, j: (0, 0)),
            pl.BlockSpec((1, 1, d), lambda i, j: (base + row(i) * N_MOD + 1, 0, 0)),
            pl.BlockSpec((1, 1, d), lambda i, j: (base + row(i) * N_MOD + 0, 0, 0)),
            pl.BlockSpec((d, tn), lambda i, j: (0, j)),
        ],
        out_specs=pl.BlockSpec((tm, tn), lambda i, j: (i, j)),
        scratch_shapes=[pltpu.VMEM((tm, d), BF16)],
        compiler_params=_params(("parallel", "arbitrary")),
        name="in_proj",
    )(x, norm_g.reshape(1, d), mod3, mod3, w_in_bf)


def _tile_flags(i, n_ctx_tiles, tiles_per_lat):
    j = jnp.maximum(i - n_ctx_tiles, 0) % tiles_per_lat
    is_ctx = i < n_ctx_tiles
    first = jnp.logical_or(is_ctx, j == 0)
    last = jnp.logical_or(is_ctx, j == tiles_per_lat - 1)
    return first, last


CONV_HALO = 16


def _conv_kernel(n_ctx_tiles, tiles_per_lat,
                 ca_ref, cb_ref, pa_ref, pb_ref, na_ref, nb_ref, w_ref, b_ref, lg_ref, lb_ref,
                 o_ref, u_ref):
    i = pl.program_id(0)
    first, last = _tile_flags(i, n_ctx_tiles, tiles_per_lat)
    t = SEQ_TILE
    hl = CONV_HALO
    u_ref[pl.ds(hl, t), :] = ca_ref[...] * jax.nn.sigmoid(cb_ref[...])
    prev = pa_ref[...] * jax.nn.sigmoid(pb_ref[...])
    nxt = na_ref[...] * jax.nn.sigmoid(nb_ref[...])
    u_ref[pl.ds(0, hl), :] = jnp.where(first, 0.0, prev)
    u_ref[pl.ds(hl + t, hl), :] = jnp.where(last, 0.0, nxt)

    pad = CONV_WIDTH // 2
    lane = 128
    cols = []
    for c in range(D_CONV // lane):
        acc = jnp.zeros((t, lane), F32)
        for k in range(CONV_WIDTH):
            acc = acc + u_ref[pl.ds(hl - pad + k, t), pl.ds(c * lane, lane)] * w_ref[pl.ds(k, 1), pl.ds(c * lane, lane)]
        cols.append(acc)
    y = jnp.concatenate(cols, axis=-1) + b_ref[...]
    mu = jnp.mean(y, axis=-1, keepdims=True)
    var = jnp.mean(jnp.square(y - mu), axis=-1, keepdims=True)
    z = (y - mu) * lax.rsqrt(var + EPS) * lg_ref[...] + lb_ref[...]
    o_ref[...] = _silu(z).astype(BF16)


def _conv_mixer(proj, conv_w, conv_b, ln_g, ln_b, n_ctx_tiles, tiles_per_lat):
    m = proj.shape[0]
    t, hl = SEQ_TILE, CONV_HALO
    r = t // hl
    n_tiles = m // t
    last_halo = m // hl - 1
    w_pad = jnp.zeros((32, D_CONV), F32).at[:CONV_WIDTH].set(conv_w)
    row = lambda v: v.reshape(1, D_CONV)
    cur = lambda c: pl.BlockSpec((t, D_CONV), lambda i: (i, c))
    prv = lambda c: pl.BlockSpec((hl, D_CONV), lambda i: (jnp.maximum(i * r - 1, 0), c))
    nxt = lambda c: pl.BlockSpec((hl, D_CONV), lambda i: (jnp.minimum((i + 1) * r, last_halo), c))
    vec = pl.BlockSpec((1, D_CONV), lambda i: (0, 0))
    return pl.pallas_call(
        functools.partial(_conv_kernel, n_ctx_tiles, tiles_per_lat),
        out_shape=jax.ShapeDtypeStruct((m, D_CONV), BF16),
        grid=(n_tiles,),
        in_specs=[cur(0), cur(1), prv(0), prv(1), nxt(0), nxt(1),
                  pl.BlockSpec((32, D_CONV), lambda i: (0, 0)), vec, vec, vec],
        out_specs=pl.BlockSpec((t, D_CONV), lambda i: (i, 0)),
        scratch_shapes=[pltpu.VMEM((t + 2 * hl, D_CONV), F32)],
        compiler_params=_params(("parallel",)),
        name="conv_mixer",
    )(proj, proj, proj, proj, proj, proj, w_pad, row(conv_b), row(ln_g), row(ln_b))


RG_HALO = 8


def _shift_rows(x, s, fill, reverse, row):
    t = x.shape[0]
    if reverse:
        return jnp.where(row < t - s, pltpu.roll(x, t - s, axis=0), fill)
    return jnp.where(row >= s, pltpu.roll(x, s, axis=0), fill)


def _rglru_kernel(reverse, with_out, n_ctx_tiles, tiles_per_lat, *refs):
    if with_out:
        (x_ref, xp_ref, xn_ref, cw_ref, cb_ref, wg_ref, bg_ref, lam_ref, h0_ref, gate_ref, hf_ref,
         o_ref, st_ref, carry_ref, ext_ref) = refs
    else:
        (x_ref, xp_ref, xn_ref, cw_ref, cb_ref, wg_ref, bg_ref, lam_ref, h0_ref,
         o_ref, st_ref, carry_ref, ext_ref) = refs
    n_tiles = pl.num_programs(0)
    step = pl.program_id(0)
    i = (n_tiles - 1 - step) if reverse else step
    first, last = _tile_flags(i, n_ctx_tiles, tiles_per_lat)
    t = SEQ_TILE

    ext_ref[pl.ds(0, RG_HALO), :] = jnp.where(first, 0.0, xp_ref[...])
    ext_ref[pl.ds(RG_HALO, t), :] = x_ref[...]
    ext_ref[pl.ds(RG_HALO + t, RG_HALO), :] = jnp.where(last, 0.0, xn_ref[...])
    xr = cb_ref[...] + jnp.zeros((t, D_RG), F32)
    for k in range(RG_CONV_WIDTH):
        off = RG_HALO - RG_CONV_WIDTH // 2 + k
        xr = xr + ext_ref[pl.ds(off, t), :] * cw_ref[pl.ds(k, 1), :]

    pre = jnp.dot(xr.astype(BF16), wg_ref[...], preferred_element_type=F32) + bg_ref[...]
    r = jax.nn.sigmoid(pre[:, :D_RG])
    ig = jax.nn.sigmoid(pre[:, D_RG:])
    nl = -lam_ref[...]
    softplus = jnp.maximum(nl, 0.0) + jnp.log1p(jnp.exp(-jnp.abs(nl)))
    log_a = -RG_C * r * softplus
    a = jnp.exp(log_a)
    u = jnp.sqrt(-jnp.tanh(log_a) * (a * a + 1.0)) * (ig * xr)

    row = lax.broadcasted_iota(jnp.int32, (t, D_RG), 0)
    s = 1
    while s < t:
        a_s = _shift_rows(a, s, 1.0, reverse, row)
        u_s = _shift_rows(u, s, 0.0, reverse, row)
        u = a * u_s + u
        a = a * a_s
        s *= 2

    seq_start = last if reverse else first
    h_in = jnp.where(seq_start, h0_ref[0], carry_ref[...])
    h = a * h_in + u
    end = h[0:1] if reverse else h[t - 1:t]
    carry_ref[...] = end
    st_ref[0] = end
    if with_out:
        o_ref[...] = (jax.nn.gelu(gate_ref[...]) * (hf_ref[...] + h)).astype(BF16)
    else:
        o_ref[...] = h


def _seq_of_tile(n_ctx_tiles, tiles_per_lat):
    def f(i):
        return jnp.where(i < n_ctx_tiles, i, n_ctx_tiles + (i - n_ctx_tiles) // tiles_per_lat)
    return f


def _rglru_dir(proj, hf, conv_w, conv_b, w_gate_bf, b_gate, lam, h0, reverse, n_ctx_tiles, tiles_per_lat):
    m = proj.shape[0]
    t, hl = SEQ_TILE, RG_HALO
    r = t // hl
    n_tiles = m // t
    last_halo = m // hl - 1
    with_out = hf is not None
    tile = (lambda s: n_tiles - 1 - s) if reverse else (lambda s: s)
    seq = _seq_of_tile(n_ctx_tiles, tiles_per_lat)
    col_x = (2 * D_CONV + D_RG) // D_RG
    col_g = (2 * D_CONV) // D_RG
    cw_pad = jnp.zeros((8, D_RG), F32).at[:RG_CONV_WIDTH].set(conv_w)
    in_specs = [
        pl.BlockSpec((t, D_RG), lambda s: (tile(s), col_x)),
        pl.BlockSpec((hl, D_RG), lambda s: (jnp.maximum(tile(s) * r - 1, 0), col_x)),
        pl.BlockSpec((hl, D_RG), lambda s: (jnp.minimum((tile(s) + 1) * r, last_halo), col_x)),
        pl.BlockSpec((8, D_RG), lambda s: (0, 0)),
        pl.BlockSpec((1, D_RG), lambda s: (0, 0)),
        pl.BlockSpec((D_RG, 2 * D_RG), lambda s: (0, 0)),
        pl.BlockSpec((1, 2 * D_RG), lambda s: (0, 0)),
        pl.BlockSpec((1, D_RG), lambda s: (0, 0)),
        pl.BlockSpec((1, 1, D_RG), lambda s: (seq(tile(s)), 0, 0)),
    ]
    args = [proj, proj, proj, cw_pad, conv_b.reshape(1, D_RG), w_gate_bf, b_gate.reshape(1, 2 * D_RG),
            lam.reshape(1, D_RG), h0]
    if with_out:
        in_specs += [pl.BlockSpec((t, D_RG), lambda s: (tile(s), col_g)),
                     pl.BlockSpec((t, D_RG), lambda s: (tile(s), 0))]
        args += [proj, hf]
    out_dtype = BF16 if with_out else F32
    return pl.pallas_call(
        functools.partial(_rglru_kernel, reverse, with_out, n_ctx_tiles, tiles_per_lat),
        out_shape=(jax.ShapeDtypeStruct((m, D_RG), out_dtype),
                   jax.ShapeDtypeStruct((n_tiles, 1, D_RG), F32)),
        grid=(n_tiles,),
        in_specs=in_specs,
        out_specs=(pl.BlockSpec((t, D_RG), lambda s: (tile(s), 0)),
                   pl.BlockSpec((1, 1, D_RG), lambda s: (tile(s), 0, 0))),
        scratch_shapes=[pltpu.VMEM((1, D_RG), F32), pltpu.VMEM((t + 2 * hl, D_RG), F32)],
        compiler_params=_params(("arbitrary",)),
        name="rglru_bwd" if reverse else "rglru_fwd",
    )(*args)


def _block_diag(w):
    g, bi, bj = w.shape
    eye = jnp.eye(g, dtype=w.dtype)
    return (eye[:, None, :, None] * w[:, :, None, :]).reshape(g * bi, g * bj)


def _ret_kernel(n_chunks, rotary, q_ref, k_ref, v_ref, g_ref, lg_ref, gn_ref, s0_ref, *rest):
    if rotary:
        cos_ref, sin_ref, o_ref, st_ref, qb_ref, kb_ref, sf_ref, sb_ref = rest
    else:
        o_ref, st_ref, qb_ref, kb_ref, sf_ref, sb_ref = rest
    c = RET_CHUNK
    dh = RET_HEAD_DIM
    scale = RET_HEAD_DIM ** -0.5

    q = q_ref[...]
    k = k_ref[...] * scale
    if rotary:
        lane = lax.broadcasted_iota(jnp.int32, q.shape, 1)
        low = (lane % (dh // 2)) < (dh // 4)
        cos = cos_ref[...]
        sin = sin_ref[...]

        def rot(x):
            partner = jnp.where(low, pltpu.roll(x, dh - dh // 4, axis=1), pltpu.roll(x, dh // 4, axis=1))
            return x * cos + partner * sin
        q = rot(q)
        k = rot(k)
    qb_ref[...] = q
    kb_ref[...] = k

    lgf = lg_ref[0]
    lgb = lg_ref[1]
    ii = lax.broadcasted_iota(jnp.int32, (c, c), 0).astype(F32)
    jj = lax.broadcasted_iota(jnp.int32, (c, c), 1).astype(F32)
    rel = ii - jj
    decay = (jnp.where(rel >= 0, jnp.exp(jnp.maximum(rel, 0.0) * lgf), 0.0)
             + jnp.where(rel <= 0, jnp.exp(jnp.maximum(-rel, 0.0) * lgb), 0.0))
    qwf = jnp.exp((ii + 1.0) * lgf)
    kwf = jnp.exp((c - 1.0 - ii) * lgf)
    qwb = jnp.exp((c - ii) * lgb)
    kwb = jnp.exp(ii * lgb)
    cdf = jnp.exp(c * lgf)
    cdb = jnp.exp(c * lgb)

    def chunk(ci):
        return pl.ds(pl.multiple_of(ci * c, c), c)

    def kv_of(ci, kw):
        kc = (kb_ref[chunk(ci), :] * kw).astype(BF16)
        vc = v_ref[chunk(ci), :].astype(BF16)
        return lax.dot_general(kc, vc, (((0,), (0,)), ((), ())), preferred_element_type=F32)

    def fwd(ci, s):
        sf_ref[ci] = s.astype(BF16)
        return cdf * s + kv_of(ci, kwf)

    def bwd(step, s):
        ci = n_chunks - 1 - step
        sb_ref[ci] = s.astype(BF16)
        return cdb * s + kv_of(ci, kwb)

    st_ref[0, 0, 0] = lax.fori_loop(0, n_chunks, fwd, s0_ref[0, 0, 0])
    st_ref[0, 1, 0] = lax.fori_loop(0, n_chunks, bwd, s0_ref[0, 1, 0])

    gn = gn_ref[...]

    def out(ci, carry):
        qf = qb_ref[chunk(ci), :]
        kc = kb_ref[chunk(ci), :].astype(BF16)
        vc = v_ref[chunk(ci), :].astype(BF16)
        sc = lax.dot_general(qf.astype(BF16), kc, (((1,), (1,)), ((), ())), preferred_element_type=F32) * decay
        lhs = jnp.concatenate([sc.astype(BF16), (qf * qwf).astype(BF16), (qf * qwb).astype(BF16)], axis=1)
        rhs = jnp.concatenate([vc, sf_ref[ci], sb_ref[ci]], axis=0)
        o = jnp.dot(lhs, rhs, preferred_element_type=F32)
        mu = jnp.mean(o, axis=-1, keepdims=True)
        var = jnp.mean(jnp.square(o - mu), axis=-1, keepdims=True)
        o = (o - mu) * lax.rsqrt(var + EPS) * gn
        o_ref[chunk(ci), :] = (_silu(g_ref[chunk(ci), :]) * o).astype(BF16)
        return carry

    lax.fori_loop(0, n_chunks, out, 0)


def _retention(proj, log_g3, gn_g, s0, tables, row_blk0, n_seq, length, layer):
    dh, nh = RET_HEAD_DIM, N_RET_HEADS
    rotary = tables is not None
    col0 = (2 * D_CONV + 2 * D_RG) // dh
    blk = lambda part: pl.BlockSpec((length, dh), lambda b, h: (row_blk0 + b, col0 + part * nh + h))
    in_specs = [blk(0), blk(1), blk(2), blk(3),
                pl.BlockSpec((2, 1, dh), lambda b, h: ((layer * nh + h), 0, 0)),
                pl.BlockSpec((1, dh), lambda b, h: (0, h)),
                pl.BlockSpec((1, 2, 1, dh, dh), lambda b, h: (b, 0, h, 0, 0))]
    args = [proj, proj, proj, proj, log_g3, gn_g.reshape(1, D_RET), s0]
    if rotary:
        tab = pl.BlockSpec((length, dh), lambda b, h: (0, 0))
        in_specs += [tab, tab]
        args += list(tables)
    n_chunks = length // RET_CHUNK
    return pl.pallas_call(
        functools.partial(_ret_kernel, n_chunks, rotary),
        out_shape=(jax.ShapeDtypeStruct((n_seq * length, D_RET), BF16),
                   jax.ShapeDtypeStruct((n_seq, 2, nh, dh, dh), F32)),
        grid=(n_seq, nh),
        in_specs=in_specs,
        out_specs=(pl.BlockSpec((length, dh), lambda b, h: (b, h)),
                   pl.BlockSpec((1, 2, 1, dh, dh), lambda b, h: (b, 0, h, 0, 0))),
        scratch_shapes=[pltpu.VMEM((length, dh), F32), pltpu.VMEM((length, dh), F32),
                        pltpu.VMEM((n_chunks, dh, dh), BF16), pltpu.VMEM((n_chunks, dh, dh), BF16)],
        compiler_params=_params(("parallel", "parallel")),
        name="retention_lat" if rotary else "retention_ctx",
    )(*args)


def _rope_tables(length):
    dh = RET_HEAD_DIM
    quarter = dh // 4
    pos = jnp.arange(length)
    rows = (pos // GRID_W).astype(F32)
    cols = (pos % GRID_W).astype(F32)
    freq = ROPE_BASE ** (-jnp.arange(quarter, dtype=F32) / quarter)
    ang_r = rows[:, None] * freq[None, :]
    ang_c = cols[:, None] * freq[None, :]
    cos = jnp.concatenate([jnp.cos(ang_r), jnp.cos(ang_r), jnp.cos(ang_c), jnp.cos(ang_c)], axis=-1)
    sin = jnp.concatenate([-jnp.sin(ang_r), jnp.sin(ang_r), -jnp.sin(ang_c), jnp.sin(ang_c)], axis=-1)
    return cos, sin


def _out_proj_kernel(u_ref, r_ref, o_ref, w_ref, x_ref, g1_ref, ng_ref, sc_ref, sh_ref, rw_ref,
                     x1_ref, h2_ref, aff_ref):
    lhs = jnp.concatenate([u_ref[...], r_ref[...], o_ref[...]], axis=-1)
    y = jnp.dot(lhs, w_ref[...], preferred_element_type=F32)
    x1 = x_ref[...] + g1_ref[0] * y
    x1_ref[...] = x1
    n = x1 * lax.rsqrt(jnp.mean(x1 * x1, axis=-1, keepdims=True) + EPS) * ng_ref[...]
    h2 = (n * (1.0 + sc_ref[0]) + sh_ref[0]).astype(BF16)
    h2_ref[...] = h2
    logits = lax.dot_general(rw_ref[...], h2, (((1,), (1,)), ((), ())), preferred_element_type=F32)
    z = logits - jnp.max(logits, axis=0, keepdims=True)
    e = jnp.exp(z)
    aff_ref[...] = e / jnp.sum(e, axis=0, keepdims=True)


def _out_proj(u, r, o, w_out_bf, x, norm_g, mod3, layer, router_wt_bf, m_ctx, lat_len):
    m, d = x.shape
    tm = 512
    row = _mod_row(m_ctx, lat_len, tm)
    base = layer * MOD_ROWS * N_MOD
    modspec = lambda j: pl.BlockSpec((1, 1, d), lambda i: (base + row(i) * N_MOD + j, 0, 0))
    return pl.pallas_call(
        _out_proj_kernel,
        out_shape=(jax.ShapeDtypeStruct((m, d), F32),
                   jax.ShapeDtypeStruct((m, d), BF16),
                   jax.ShapeDtypeStruct((N_EXPERTS, m), F32)),
        grid=(m // tm,),
        in_specs=[
            pl.BlockSpec((tm, D_CONV), lambda i: (i, 0)),
            pl.BlockSpec((tm, D_RG), lambda i: (i, 0)),
            pl.BlockSpec((tm, D_RET), lambda i: (i, 0)),
            pl.BlockSpec((d, d), lambda i: (0, 0)),
            pl.BlockSpec((tm, d), lambda i: (i, 0)),
            modspec(2),
            pl.BlockSpec((1, d), lambda i: (0, 0)),
            modspec(4), modspec(3),
            pl.BlockSpec((N_EXPERTS, d), lambda i: (0, 0)),
        ],
        out_specs=(pl.BlockSpec((tm, d), lambda i: (i, 0)),
                   pl.BlockSpec((tm, d), lambda i: (i, 0)),
                   pl.BlockSpec((N_EXPERTS, tm), lambda i: (0, i))),
        compiler_params=_params(("parallel",)),
        name="out_proj",
    )(u, r, o, w_out_bf, x, mod3, norm_g.reshape(1, d), mod3, mod3, router_wt_bf)


def _ffn_up_kernel(x_ref, wg_ref, wu_ref, h_ref, wgb_ref, wub_ref):
    @pl.when(pl.program_id(2) == 0)
    def _():
        wgb_ref[...] = wg_ref[0, 0].astype(BF16)
        wub_ref[...] = wu_ref[0, 0].astype(BF16)

    x = x_ref[0]
    g = jnp.dot(x, wgb_ref[...], preferred_element_type=F32)
    u = jnp.dot(x, wub_ref[...], preferred_element_type=F32)
    h_ref[0] = (_silu(g) * u).astype(BF16)


def _ffn_up(xg, w_gate, w_up, layer):
    e, me, d = xg.shape
    f = w_gate.shape[-1]
    tm, tn = me // 4, 512
    wspec = pl.BlockSpec((1, 1, d, tn), lambda ei, j, i: (layer, ei, 0, j))
    return pl.pallas_call(
        _ffn_up_kernel,
        out_shape=jax.ShapeDtypeStruct((e, me, f), BF16),
        grid=(e, f // tn, me // tm),
        in_specs=[pl.BlockSpec((1, tm, d), lambda ei, j, i: (ei, i, 0)), wspec, wspec],
        out_specs=pl.BlockSpec((1, tm, tn), lambda ei, j, i: (ei, i, j)),
        scratch_shapes=[pltpu.VMEM((d, tn), BF16), pltpu.VMEM((d, tn), BF16)],
        compiler_params=_params(("parallel", "parallel", "arbitrary")),
        name="ffn_up",
    )(xg, w_gate, w_up)


def _ffn_down_kernel(h_ref, w_ref, y_ref, wb_ref):
    @pl.when(pl.program_id(2) == 0)
    def _():
        wb_ref[...] = w_ref[0, 0].astype(BF16)

    y_ref[0] = jnp.dot(h_ref[0], wb_ref[...], preferred_element_type=F32)


def _ffn_down(hid, w_down, layer):
    e, me, f = hid.shape
    d = w_down.shape[-1]
    tm, tn = me // 8, 512
    return pl.pallas_call(
        _ffn_down_kernel,
        out_shape=jax.ShapeDtypeStruct((e, me, d), F32),
        grid=(e, d // tn, me // tm),
        in_specs=[pl.BlockSpec((1, tm, f), lambda ei, j, i: (ei, i, 0)),
                  pl.BlockSpec((1, 1, f, tn), lambda ei, j, i: (layer, ei, 0, j))],
        out_specs=pl.BlockSpec((1, tm, tn), lambda ei, j, i: (ei, i, j)),
        scratch_shapes=[pltpu.VMEM((f, tn), BF16)],
        compiler_params=_params(("parallel", "parallel", "arbitrary")),
        name="ffn_down",
    )(hid, w_down)


def _residual_kernel(final, x_ref, y_ref, g2_ref, fg_ref, o_ref):
    x2 = x_ref[...] + g2_ref[0] * y_ref[...]
    if final:
        x2 = x2 * lax.rsqrt(jnp.mean(x2 * x2, axis=-1, keepdims=True) + EPS) * fg_ref[...]
    o_ref[...] = x2


def _residual(x1, moe, mod3, layer, final_g, final, m_ctx, lat_len):
    m, d = x1.shape
    tm = 512
    row = _mod_row(m_ctx, lat_len, tm)
    base = layer * MOD_ROWS * N_MOD
    return pl.pallas_call(
        functools.partial(_residual_kernel, final),
        out_shape=jax.ShapeDtypeStruct((m, d), F32),
        grid=(m // tm,),
        in_specs=[pl.BlockSpec((tm, d), lambda i: (i, 0)),
                  pl.BlockSpec((tm, d), lambda i: (i, 0)),
                  pl.BlockSpec((1, 1, d), lambda i: (base + row(i) * N_MOD + 5, 0, 0)),
                  pl.BlockSpec((1, d), lambda i: (0, 0))],
        out_specs=pl.BlockSpec((tm, d), lambda i: (i, 0)),
        compiler_params=_params(("parallel",)),
        name="residual",
    )(x1, moe, mod3, final_g.reshape(1, d))


def _route(aff_t, n_seq, length, row0):
    cap = EC_FACTOR * length // N_EXPERTS
    a = aff_t[:, row0:row0 + n_seq * length].reshape(N_EXPERTS, n_seq, length)
    gate, idx = lax.top_k(a, cap)
    rows = row0 + jnp.arange(n_seq)[None, :, None] * length + idx
    return rows.reshape(N_EXPERTS, n_seq * cap), gate.reshape(N_EXPERTS, n_seq * cap)


def kernel(x_prompt, x_sample, state_rglru, state_ret, c, c_ctx, norm1_g, norm2_g, w_mod, b_mod, w_in, conv_w, conv_b, conv_ln_g, conv_ln_b, rg_conv_w, rg_conv_b, rg_w_a, rg_b_a, rg_w_x, rg_b_x, rg_lambda, ret_decay, ret_gn_g, w_out, router_w, exp_w_gate, exp_w_up, exp_w_down, final_g):
    depth = w_in.shape[0]
    n_ctx, ctx_len, d = x_prompt.shape
    n_lat, lat_len, _ = x_sample.shape
    m_ctx, m_lat = n_ctx * ctx_len, n_lat * lat_len
    assert ctx_len == SEQ_TILE and lat_len % SEQ_TILE == 0 and m_ctx % lat_len == 0
    n_ctx_tiles = m_ctx // SEQ_TILE
    tiles_per_lat = lat_len // SEQ_TILE
    n_seq = n_ctx + n_lat

    x = jnp.concatenate([x_prompt.reshape(m_ctx, d), x_sample.reshape(m_lat, d)], axis=0)

    cond = jnp.zeros((MOD_ROWS, d), F32).at[0].set(c_ctx).at[1:1 + n_lat].set(c)
    mod = _modulation(cond, w_mod, b_mod)
    mod3 = mod.reshape(depth * MOD_ROWS * N_MOD, 1, d)

    tables = _rope_tables(lat_len)
    log_g = jax.nn.log_sigmoid(ret_decay.astype(F32))
    log_g3 = jnp.broadcast_to(
        jnp.swapaxes(log_g, 1, 2).reshape(depth * N_RET_HEADS * 2, 1, 1), (depth * N_RET_HEADS * 2, 1, RET_HEAD_DIM))
    zero_ret = jnp.zeros((n_ctx, 2, N_RET_HEADS, RET_HEAD_DIM, RET_HEAD_DIM), F32)

    rg_states, ret_states = [], []
    for l in range(depth):
        proj = _in_proj(x, norm1_g[l], mod3, l, w_in[l].astype(BF16), m_ctx, lat_len)

        u = _conv_mixer(proj, conv_w[l], conv_b[l], conv_ln_g[l], conv_ln_b[l], n_ctx_tiles, tiles_per_lat)

        h0 = jnp.concatenate([jnp.zeros((n_ctx, 2, D_RG), F32), state_rglru[:, l]], axis=0)
        wg = [jnp.concatenate([_block_diag(rg_w_a[l, dd]), _block_diag(rg_w_x[l, dd])], axis=1).astype(BF16)
              for dd in range(2)]
        bg = [jnp.concatenate([rg_b_a[l, dd], rg_b_x[l, dd]]) for dd in range(2)]
        hf, st_f = _rglru_dir(proj, None, rg_conv_w[l], rg_conv_b[l], wg[0], bg[0], rg_lambda[l, 0],
                              h0[:, 0].reshape(n_seq, 1, D_RG), False, n_ctx_tiles, tiles_per_lat)
        r_out, st_b = _rglru_dir(proj, hf, rg_conv_w[l], rg_conv_b[l], wg[1], bg[1], rg_lambda[l, 1],
                                 h0[:, 1].reshape(n_seq, 1, D_RG), True, n_ctx_tiles, tiles_per_lat)
        rg_states.append(jnp.stack([st_f[:n_ctx, 0], st_b[:n_ctx, 0]], axis=1))

        lg_l = log_g3[l * N_RET_HEADS * 2:(l + 1) * N_RET_HEADS * 2]
        o_ctx, s_ctx = _retention(proj, lg_l, ret_gn_g[l], zero_ret, None, 0, n_ctx, ctx_len, 0)
        o_lat, _ = _retention(proj, lg_l, ret_gn_g[l], state_ret[:, l], tables, m_ctx // lat_len, n_lat, lat_len, 0)
        ret_states.append(s_ctx)
        o = jnp.concatenate([o_ctx, o_lat], axis=0)

        x1, h2, aff_t = _out_proj(u, r_out, o, w_out[l].astype(BF16), x, norm2_g[l], mod3, l,
                                  router_w[l].T.astype(BF16), m_ctx, lat_len)

        rows_c, gate_c = _route(aff_t, n_ctx, ctx_len, 0)
        rows_l, gate_l = _route(aff_t, n_lat, lat_len, m_ctx)
        rows = jnp.concatenate([rows_c, rows_l], axis=1)
        gates = jnp.concatenate([gate_c, gate_l], axis=1)
        xg = h2[rows]
        hid = _ffn_up(xg, exp_w_gate, exp_w_up, l)
        y = _ffn_down(hid, exp_w_down, l)
        moe = jnp.zeros((m_ctx + m_lat, d), F32).at[rows.reshape(-1)].add(
            (y * gates[..., None]).reshape(-1, d))

        x = _residual(x1, moe, mod3, l, final_g, l == depth - 1, m_ctx, lat_len)

    y_prompt = x[:m_ctx].reshape(n_ctx, ctx_len, d)
    y_sample = x[m_ctx:].reshape(n_lat, lat_len, d)
    new_state_rglru = jnp.stack(rg_states, axis=1)
    new_state_ret = jnp.stack(ret_states, axis=1)
    return (y_prompt, y_sample, new_state_rglru, new_state_ret)
```

```python
import functools

import jax
import jax.numpy as jnp
from jax import lax
from jax.experimental import pallas as pl
from jax.experimental.pallas import tpu as pltpu

F32 = jnp.float32
BF16 = jnp.bfloat16

D_MODEL = 2048
D_CONV = 512
CONV_WIDTH = 31
D_RG = 512
RG_BLOCKS = 8
RG_CONV_WIDTH = 4
RG_C = 8.0
N_RET_HEADS = 8
RET_HEAD_DIM = 128
D_RET = N_RET_HEADS * RET_HEAD_DIM
RET_CHUNK = 128
ROPE_BASE = 10000.0
D_IN = 2 * D_CONV + 2 * D_RG + 4 * D_RET
N_EXPERTS = 16
EC_FACTOR = 2
N_MOD = 6
EPS = 1e-6
GRID_W = 64

V7X_VMEM_LIMIT_BYTES = 56 * 1024 * 1024
SEQ_TILE = 256
MOD_ROWS = 16


def _params(sem, vmem=V7X_VMEM_LIMIT_BYTES):
    return pltpu.CompilerParams(dimension_semantics=sem, vmem_limit_bytes=vmem)


def _silu(x):
    return x * jax.nn.sigmoid(x)


def _mod_kernel(c_ref, w_ref, b_ref, o_ref):
    c = _silu(c_ref[...]).astype(BF16)
    o_ref[0] = jnp.dot(c, w_ref[0].astype(BF16), preferred_element_type=F32) + b_ref[0]


def _modulation(cond, w_mod, b_mod):
    depth, d, n = w_mod.shape
    tn = 1024
    return pl.pallas_call(
        _mod_kernel,
        out_shape=jax.ShapeDtypeStruct((depth, MOD_ROWS, n), F32),
        grid=(depth, n // tn),
        in_specs=[
            pl.BlockSpec((MOD_ROWS, d), lambda l, j: (0, 0)),
            pl.BlockSpec((1, d, tn), lambda l, j: (l, 0, j)),
            pl.BlockSpec((1, 1, tn), lambda l, j: (l, 0, j)),
        ],
        out_specs=pl.BlockSpec((1, MOD_ROWS, tn), lambda l, j: (l, 0, j)),
        compiler_params=_params(("parallel", "parallel")),
        name="modulation",
    )(cond, w_mod, b_mod.reshape(depth, 1, n))


def _mod_row(m_ctx, seq_rows, tm):
    def f(i):
        start = i * tm
        return jnp.where(start < m_ctx, 0, 1 + (start - m_ctx) // seq_rows)
    return f


def _in_proj_kernel(x_ref, g_ref, sc_ref, sh_ref, w_ref, o_ref, h_ref):
    @pl.when(pl.program_id(1) == 0)
    def _():
        x = x_ref[...]
        y = x * lax.rsqrt(jnp.mean(x * x, axis=-1, keepdims=True) + EPS) * g_ref[...]
        h_ref[...] = (y * (1.0 + sc_ref[0]) + sh_ref[0]).astype(BF16)

    o_ref[...] = jnp.dot(h_ref[...], w_ref[...], preferred_element_type=F32)


def _in_proj(x, norm_g, mod3, layer, w_in_bf, m_ctx, lat_len):
    m, d = x.shape
    n = w_in_bf.shape[-1]
    tm, tn = 1024, 1024
    row = _mod_row(m_ctx, lat_len, tm)
    base = layer * MOD_ROWS * N_MOD
    return pl.pallas_call(
        _in_proj_kernel,
        out_shape=jax.ShapeDtypeStruct((m, n), F32),
        grid=(m // tm, n // tn),
        in_specs=[
            pl.BlockSpec((tm, d), lambda i, j: (i, 0)),
            pl.BlockSpec((1, d), lambda i, j: (0, 0)),
            pl.BlockSpec((1, 1, d), lambda i, j: (base + row(i) * N_MOD + 1, 0, 0)),
            pl.BlockSpec((1, 1, d), lambda i, j: (base + row(i) * N_MOD + 0, 0, 0)),
            pl.BlockSpec((d, tn), lambda i, j: (0, j)),
        ],
        out_specs=pl.BlockSpec((tm, tn), lambda i, j: (i, j)),
        scratch_shapes=[pltpu.VMEM((tm, d), BF16)],
        compiler_params=_params(("parallel", "arbitrary")),
        name="in_proj",
    )(x, norm_g.reshape(1, d), mod3, mod3, w_in_bf)


def _tile_flags(i, n_ctx_tiles, tiles_per_lat):
    j = jnp.maximum(i - n_ctx_tiles, 0) % tiles_per_lat
    is_ctx = i < n_ctx_tiles
    first = jnp.logical_or(is_ctx, j == 0)
    last = jnp.logical_or(is_ctx, j == tiles_per_lat - 1)
    return first, last


CONV_HALO = 16


def _conv_kernel(n_ctx_tiles, tiles_per_lat,
                 ca_ref, cb_ref, pa_ref, pb_ref, na_ref, nb_ref, w_ref, b_ref, lg_ref, lb_ref,
                 o_ref, u_ref):
    i = pl.program_id(0)
    first, last = _tile_flags(i, n_ctx_tiles, tiles_per_lat)
    t = SEQ_TILE
    hl = CONV_HALO
    u_ref[pl.ds(hl, t), :] = ca_ref[...] * jax.nn.sigmoid(cb_ref[...])
    prev = pa_ref[...] * jax.nn.sigmoid(pb_ref[...])
    nxt = na_ref[...] * jax.nn.sigmoid(nb_ref[...])
    u_ref[pl.ds(0, hl), :] = jnp.where(first, 0.0, prev)
    u_ref[pl.ds(hl + t, hl), :] = jnp.where(last, 0.0, nxt)

    pad = CONV_WIDTH // 2
    lane = 128
    cols = []
    for c in range(D_CONV // lane):
        acc = jnp.zeros((t, lane), F32)
        for k in range(CONV_WIDTH):
            acc = acc + u_ref[pl.ds(hl - pad + k, t), pl.ds(c * lane, lane)] * w_ref[pl.ds(k, 1), pl.ds(c * lane, lane)]
        cols.append(acc)
    y = jnp.concatenate(cols, axis=-1) + b_ref[...]
    mu = jnp.mean(y, axis=-1, keepdims=True)
    var = jnp.mean(jnp.square(y - mu), axis=-1, keepdims=True)
    z = (y - mu) * lax.rsqrt(var + EPS) * lg_ref[...] + lb_ref[...]
    o_ref[...] = _silu(z).astype(BF16)


def _conv_mixer(proj, conv_w, conv_b, ln_g, ln_b, n_ctx_tiles, tiles_per_lat):
    m = proj.shape[0]
    t, hl = SEQ_TILE, CONV_HALO
    r = t // hl
    n_tiles = m // t
    last_halo = m // hl - 1
    w_pad = jnp.zeros((32, D_CONV), F32).at[:CONV_WIDTH].set(conv_w)
    row = lambda v: v.reshape(1, D_CONV)
    cur = lambda c: pl.BlockSpec((t, D_CONV), lambda i: (i, c))
    prv = lambda c: pl.BlockSpec((hl, D_CONV), lambda i: (jnp.maximum(i * r - 1, 0), c))
    nxt = lambda c: pl.BlockSpec((hl, D_CONV), lambda i: (jnp.minimum((i + 1) * r, last_halo), c))
    vec = pl.BlockSpec((1, D_CONV), lambda i: (0, 0))
    return pl.pallas_call(
        functools.partial(_conv_kernel, n_ctx_tiles, tiles_per_lat),
        out_shape=jax.ShapeDtypeStruct((m, D_CONV), BF16),
        grid=(n_tiles,),
        in_specs=[cur(0), cur(1), prv(0), prv(1), nxt(0), nxt(1),
                  pl.BlockSpec((32, D_CONV), lambda i: (0, 0)), vec, vec, vec],
        out_specs=pl.BlockSpec((t, D_CONV), lambda i: (i, 0)),
        scratch_shapes=[pltpu.VMEM((t + 2 * hl, D_CONV), F32)],
        compiler_params=_params(("parallel",)),
        name="conv_mixer",
    )(proj, proj, proj, proj, proj, proj, w_pad, row(conv_b), row(ln_g), row(ln_b))


RG_HALO = 8


def _shift_rows(x, s, fill, reverse, row):
    t = x.shape[0]
    if reverse:
        return jnp.where(row < t - s, pltpu.roll(x, t - s, axis=0), fill)
    return jnp.where(row >= s, pltpu.roll(x, s, axis=0), fill)


def _rglru_kernel(reverse, with_out, n_ctx_tiles, tiles_per_lat, *refs):
    if with_out:
        (x_ref, xp_ref, xn_ref, cw_ref, cb_ref, wg_ref, bg_ref, lam_ref, h0_ref, gate_ref, hf_ref,
         o_ref, st_ref, carry_ref, ext_ref) = refs
    else:
        (x_ref, xp_ref, xn_ref, cw_ref, cb_ref, wg_ref, bg_ref, lam_ref, h0_ref,
         o_ref, st_ref, carry_ref, ext_ref) = refs
    n_tiles = pl.num_programs(0)
    step = pl.program_id(0)
    i = (n_tiles - 1 - step) if reverse else step
    first, last = _tile_flags(i, n_ctx_tiles, tiles_per_lat)
    t = SEQ_TILE

    ext_ref[pl.ds(0, RG_HALO), :] = jnp.where(first, 0.0, xp_ref[...])
    ext_ref[pl.ds(RG_HALO, t), :] = x_ref[...]
    ext_ref[pl.ds(RG_HALO + t, RG_HALO), :] = jnp.where(last, 0.0, xn_ref[...])
    xr = cb_ref[...] + jnp.zeros((t, D_RG), F32)
    for k in range(RG_CONV_WIDTH):
        off = RG_HALO - RG_CONV_WIDTH // 2 + k
        xr = xr + ext_ref[pl.ds(off, t), :] * cw_ref[pl.ds(k, 1), :]

    pre = jnp.dot(xr.astype(BF16), wg_ref[...], preferred_element_type=F32) + bg_ref[...]
    r = jax.nn.sigmoid(pre[:, :D_RG])
    ig = jax.nn.sigmoid(pre[:, D_RG:])
    nl = -lam_ref[...]
    softplus = jnp.maximum(nl, 0.0) + jnp.log1p(jnp.exp(-jnp.abs(nl)))
    log_a = -RG_C * r * softplus
    a = jnp.exp(log_a)
    u = jnp.sqrt(-jnp.tanh(log_a) * (a * a + 1.0)) * (ig * xr)

    row = lax.broadcasted_iota(jnp.int32, (t, D_RG), 0)
    s = 1
    while s < t:
        a_s = _shift_rows(a, s, 1.0, reverse, row)
        u_s = _shift_rows(u, s, 0.0, reverse, row)
        u = a * u_s + u
        a = a * a_s
        s *= 2

    seq_start = last if reverse else first
    h_in = jnp.where(seq_start, h0_ref[0], carry_ref[...])
    h = a * h_in + u
    end = h[0:1] if reverse else h[t - 1:t]
    carry_ref[...] = end
    st_ref[0] = end
    if with_out:
        o_ref[...] = (jax.nn.gelu(gate_ref[...]) * (hf_ref[...] + h)).astype(BF16)
    else:
        o_ref[...] = h


def _seq_of_tile(n_ctx_tiles, tiles_per_lat):
    def f(i):
        return jnp.where(i < n_ctx_tiles, i, n_ctx_tiles + (i - n_ctx_tiles) // tiles_per_lat)
    return f


def _rglru_dir(proj, hf, conv_w, conv_b, w_gate_bf, b_gate, lam, h0, reverse, n_ctx_tiles, tiles_per_lat):
    m = proj.shape[0]
    t, hl = SEQ_TILE, RG_HALO
    r = t // hl
    n_tiles = m // t
    last_halo = m // hl - 1
    with_out = hf is not None
    tile = (lambda s: n_tiles - 1 - s) if reverse else (lambda s: s)
    seq = _seq_of_tile(n_ctx_tiles, tiles_per_lat)
    col_x = (2 * D_CONV + D_RG) // D_RG
    col_g = (2 * D_CONV) // D_RG
    cw_pad = jnp.zeros((8, D_RG), F32).at[:RG_CONV_WIDTH].set(conv_w)
    in_specs = [
        pl.BlockSpec((t, D_RG), lambda s: (tile(s), col_x)),
        pl.BlockSpec((hl, D_RG), lambda s: (jnp.maximum(tile(s) * r - 1, 0), col_x)),
        pl.BlockSpec((hl, D_RG), lambda s: (jnp.minimum((tile(s) + 1) * r, last_halo), col_x)),
        pl.BlockSpec((8, D_RG), lambda s: (0, 0)),
        pl.BlockSpec((1, D_RG), lambda s: (0, 0)),
        pl.BlockSpec((D_RG, 2 * D_RG), lambda s: (0, 0)),
        pl.BlockSpec((1, 2 * D_RG), lambda s: (0, 0)),
        pl.BlockSpec((1, D_RG), lambda s: (0, 0)),
        pl.BlockSpec((1, 1, D_RG), lambda s: (seq(tile(s)), 0, 0)),
    ]
    args = [proj, proj, proj, cw_pad, conv_b.reshape(1, D_RG), w_gate_bf, b_gate.reshape(1, 2 * D_RG),
            lam.reshape(1, D_RG), h0]
    if with_out:
        in_specs += [pl.BlockSpec((t, D_RG), lambda s: (tile(s), col_g)),
                     pl.BlockSpec((t, D_RG), lambda s: (tile(s), 0))]
        args += [proj, hf]
    out_dtype = BF16 if with_out else F32
    return pl.pallas_call(
        functools.partial(_rglru_kernel, reverse, with_out, n_ctx_tiles, tiles_per_lat),
        out_shape=(jax.ShapeDtypeStruct((m, D_RG), out_dtype),
                   jax.ShapeDtypeStruct((n_tiles, 1, D_RG), F32)),
        grid=(n_tiles,),
        in_specs=in_specs,
        out_specs=(pl.BlockSpec((t, D_RG), lambda s: (tile(s), 0)),
                   pl.BlockSpec((1, 1, D_RG), lambda s: (tile(s), 0, 0))),
        scratch_shapes=[pltpu.VMEM((1, D_RG), F32), pltpu.VMEM((t + 2 * hl, D_RG), F32)],
        compiler_params=_params(("arbitrary",)),
        name="rglru_bwd" if reverse else "rglru_fwd",
    )(*args)


def _block_diag(w):
    g, bi, bj = w.shape
    eye = jnp.eye(g, dtype=w.dtype)
    return (eye[:, None, :, None] * w[:, :, None, :]).reshape(g * bi, g * bj)


RET_GROUP = 4


def _ret_kernel(n_chunks, rotary, q_ref, k_ref, v_ref, g_ref, lg_ref, gn_ref, s0_ref, *rest):
    if rotary:
        cos_ref, sin_ref, o_ref, st_ref, qb_ref, kb_ref, sf_ref, sb_ref, kvf_ref, kvb_ref = rest
    else:
        o_ref, st_ref, qb_ref, kb_ref, sf_ref, sb_ref, kvf_ref, kvb_ref = rest
    c = RET_CHUNK
    dh = RET_HEAD_DIM
    scale = RET_HEAD_DIM ** -0.5

    q = q_ref[...]
    k = k_ref[...] * scale
    if rotary:
        lane = lax.broadcasted_iota(jnp.int32, q.shape, 1)
        low = (lane % (dh // 2)) < (dh // 4)
        cos = cos_ref[...]
        sin = sin_ref[...]

        def rot(x):
            partner = jnp.where(low, pltpu.roll(x, dh - dh // 4, axis=1), pltpu.roll(x, dh // 4, axis=1))
            return x * cos + partner * sin
        q = rot(q)
        k = rot(k)
    qb_ref[...] = q
    kb_ref[...] = k

    lgf = lg_ref[0]
    lgb = lg_ref[1]
    ii = lax.broadcasted_iota(jnp.int32, (c, c), 0).astype(F32)
    jj = lax.broadcasted_iota(jnp.int32, (c, c), 1).astype(F32)
    rel = ii - jj
    decay = (jnp.where(rel >= 0, jnp.exp(jnp.maximum(rel, 0.0) * lgf), 0.0)
             + jnp.where(rel <= 0, jnp.exp(jnp.maximum(-rel, 0.0) * lgb), 0.0))
    qwf = jnp.exp((ii + 1.0) * lgf)
    kwf = jnp.exp((c - 1.0 - ii) * lgf)
    qwb = jnp.exp((c - ii) * lgb)
    kwb = jnp.exp(ii * lgb)
    cdf = jnp.exp(c * lgf)
    cdb = jnp.exp(c * lgb)

    def chunk(ci):
        return pl.ds(pl.multiple_of(ci * c, c), c)

    group = min(RET_GROUP, n_chunks)

    def kv_group(gi, carry):
        for j in range(group):
            ci = gi * group + j
            kc = kb_ref[chunk(ci), :]
            vc = v_ref[chunk(ci), :].astype(BF16)
            dims = (((0,), (0,)), ((), ()))
            kvf_ref[ci] = lax.dot_general((kc * kwf).astype(BF16), vc, dims, preferred_element_type=F32)
            kvb_ref[ci] = lax.dot_general((kc * kwb).astype(BF16), vc, dims, preferred_element_type=F32)
        return carry

    lax.fori_loop(0, n_chunks // group, kv_group, 0)

    def fwd(ci, s):
        sf_ref[ci] = s.astype(BF16)
        return cdf * s + kvf_ref[ci]

    def bwd(step, s):
        ci = n_chunks - 1 - step
        sb_ref[ci] = s.astype(BF16)
        return cdb * s + kvb_ref[ci]

    st_ref[0, 0, 0] = lax.fori_loop(0, n_chunks, fwd, s0_ref[0, 0, 0])
    st_ref[0, 1, 0] = lax.fori_loop(0, n_chunks, bwd, s0_ref[0, 1, 0])

    gn = gn_ref[...]

    def out_group(gi, carry):
        for j in range(group):
            ci = gi * group + j
            qf = qb_ref[chunk(ci), :]
            kc = kb_ref[chunk(ci), :].astype(BF16)
            vc = v_ref[chunk(ci), :].astype(BF16)
            sc = lax.dot_general(qf.astype(BF16), kc, (((1,), (1,)), ((), ())), preferred_element_type=F32) * decay
            lhs = jnp.concatenate([sc.astype(BF16), (qf * qwf).astype(BF16), (qf * qwb).astype(BF16)], axis=1)
            rhs = jnp.concatenate([vc, sf_ref[ci], sb_ref[ci]], axis=0)
            o = jnp.dot(lhs, rhs, preferred_element_type=F32)
            mu = jnp.mean(o, axis=-1, keepdims=True)
            var = jnp.mean(jnp.square(o - mu), axis=-1, keepdims=True)
            o = (o - mu) * lax.rsqrt(var + EPS) * gn
            o_ref[chunk(ci), :] = (_silu(g_ref[chunk(ci), :]) * o).astype(BF16)
        return carry

    lax.fori_loop(0, n_chunks // group, out_group, 0)


def _retention(proj, log_g3, gn_g, s0, tables, row_blk0, n_seq, length, layer):
    dh, nh = RET_HEAD_DIM, N_RET_HEADS
    rotary = tables is not None
    col0 = (2 * D_CONV + 2 * D_RG) // dh
    blk = lambda part: pl.BlockSpec((length, dh), lambda b, h: (row_blk0 + b, col0 + part * nh + h))
    in_specs = [blk(0), blk(1), blk(2), blk(3),
                pl.BlockSpec((2, 1, dh), lambda b, h: ((layer * nh + h), 0, 0)),
                pl.BlockSpec((1, dh), lambda b, h: (0, h)),
                pl.BlockSpec((1, 2, 1, dh, dh), lambda b, h: (b, 0, h, 0, 0))]
    args = [proj, proj, proj, proj, log_g3, gn_g.reshape(1, D_RET), s0]
    if rotary:
        tab = pl.BlockSpec((length, dh), lambda b, h: (0, 0))
        in_specs += [tab, tab]
        args += list(tables)
    n_chunks = length // RET_CHUNK
    return pl.pallas_call(
        functools.partial(_ret_kernel, n_chunks, rotary),
        out_shape=(jax.ShapeDtypeStruct((n_seq * length, D_RET), BF16),
                   jax.ShapeDtypeStruct((n_seq, 2, nh, dh, dh), F32)),
        grid=(n_seq, nh),
        in_specs=in_specs,
        out_specs=(pl.BlockSpec((length, dh), lambda b, h: (b, h)),
                   pl.BlockSpec((1, 2, 1, dh, dh), lambda b, h: (b, 0, h, 0, 0))),
        scratch_shapes=[pltpu.VMEM((length, dh), F32), pltpu.VMEM((length, dh), F32),
                        pltpu.VMEM((n_chunks, dh, dh), BF16), pltpu.VMEM((n_chunks, dh, dh), BF16),
                        pltpu.VMEM((n_chunks, dh, dh), F32), pltpu.VMEM((n_chunks, dh, dh), F32)],
        compiler_params=_params(("parallel", "parallel")),
        name="retention_lat" if rotary else "retention_ctx",
    )(*args)


def _rope_tables(length):
    dh = RET_HEAD_DIM
    quarter = dh // 4
    pos = jnp.arange(length)
    rows = (pos // GRID_W).astype(F32)
    cols = (pos % GRID_W).astype(F32)
    freq = ROPE_BASE ** (-jnp.arange(quarter, dtype=F32) / quarter)
    ang_r = rows[:, None] * freq[None, :]
    ang_c = cols[:, None] * freq[None, :]
    cos = jnp.concatenate([jnp.cos(ang_r), jnp.cos(ang_r), jnp.cos(ang_c), jnp.cos(ang_c)], axis=-1)
    sin = jnp.concatenate([-jnp.sin(ang_r), jnp.sin(ang_r), -jnp.sin(ang_c), jnp.sin(ang_c)], axis=-1)
    return cos, sin


LANES = 128
PACK_ROWS = D_MODEL // (2 * LANES)
CHUNK_ROWS = D_MODEL // LANES
HIGH_HALF = 0xFFFF0000


def _pack_bf16_pairs(x):
    half = x.shape[1] // 2
    lo = lax.bitcast_convert_type(x[:, :half].astype(BF16).astype(F32), jnp.uint32) >> 16
    hi = lax.bitcast_convert_type(x[:, half:].astype(BF16).astype(F32), jnp.uint32) & jnp.uint32(HIGH_HALF)
    return lo | hi


def _unpack_low(p):
    return lax.bitcast_convert_type(p << 16, F32)


def _unpack_high(p):
    return lax.bitcast_convert_type(p & jnp.uint32(HIGH_HALF), F32)


def _store_token_rows(ref, v):
    tm, n = v.shape[0], v.shape[1] // LANES
    for j in range(n):
        ref[pl.ds(j, tm, stride=n), :] = v[:, j * LANES:(j + 1) * LANES]


def _load_token_rows(ref, tm, n):
    return jnp.concatenate([ref[pl.ds(j, tm, stride=n), :] for j in range(n)], axis=-1)


def _out_proj_kernel(u_ref, r_ref, o_ref, w_ref, x_ref, g1_ref, ng_ref, sc_ref, sh_ref, rw_ref,
                     x1_ref, h2_ref, aff_ref):
    lhs = jnp.concatenate([u_ref[...], r_ref[...], o_ref[...]], axis=-1)
    y = jnp.dot(lhs, w_ref[...], preferred_element_type=F32)
    x1 = x_ref[...] + g1_ref[0] * y
    x1_ref[...] = x1
    n = x1 * lax.rsqrt(jnp.mean(x1 * x1, axis=-1, keepdims=True) + EPS) * ng_ref[...]
    h2f = n * (1.0 + sc_ref[0]) + sh_ref[0]
    _store_token_rows(h2_ref, _pack_bf16_pairs(h2f))
    h2 = h2f.astype(BF16)
    logits = lax.dot_general(rw_ref[...], h2, (((1,), (1,)), ((), ())), preferred_element_type=F32)
    z = logits - jnp.max(logits, axis=0, keepdims=True)
    e = jnp.exp(z)
    aff_ref[...] = e / jnp.sum(e, axis=0, keepdims=True)


def _out_proj(u, r, o, w_out_bf, x, norm_g, mod3, layer, router_wt_bf, m_ctx, lat_len):
    m, d = x.shape
    tm = 512
    row = _mod_row(m_ctx, lat_len, tm)
    base = layer * MOD_ROWS * N_MOD
    modspec = lambda j: pl.BlockSpec((1, 1, d), lambda i: (base + row(i) * N_MOD + j, 0, 0))
    return pl.pallas_call(
        _out_proj_kernel,
        out_shape=(jax.ShapeDtypeStruct((m, d), F32),
                   jax.ShapeDtypeStruct((m * PACK_ROWS, LANES), jnp.uint32),
                   jax.ShapeDtypeStruct((N_EXPERTS, m), F32)),
        grid=(m // tm,),
        in_specs=[
            pl.BlockSpec((tm, D_CONV), lambda i: (i, 0)),
            pl.BlockSpec((tm, D_RG), lambda i: (i, 0)),
            pl.BlockSpec((tm, D_RET), lambda i: (i, 0)),
            pl.BlockSpec((d, d), lambda i: (0, 0)),
            pl.BlockSpec((tm, d), lambda i: (i, 0)),
            modspec(2),
            pl.BlockSpec((1, d), lambda i: (0, 0)),
            modspec(4), modspec(3),
            pl.BlockSpec((N_EXPERTS, d), lambda i: (0, 0)),
        ],
        out_specs=(pl.BlockSpec((tm, d), lambda i: (i, 0)),
                   pl.BlockSpec((tm * PACK_ROWS, LANES), lambda i: (i, 0)),
                   pl.BlockSpec((N_EXPERTS, tm), lambda i: (0, i))),
        compiler_params=_params(("parallel",)),
        name="out_proj",
    )(u, r, o, w_out_bf, x, mod3, norm_g.reshape(1, d), mod3, mod3, router_wt_bf)


def _ffn_up_kernel(x_ref, wg_ref, wu_ref, h_ref, wgb_ref, wub_ref):
    @pl.when(pl.program_id(2) == 0)
    def _():
        wgb_ref[...] = wg_ref[0, 0].astype(BF16)
        wub_ref[...] = wu_ref[0, 0].astype(BF16)

    tm = h_ref.shape[1]
    p = _load_token_rows(x_ref.at[0], tm, PACK_ROWS)
    x = jnp.concatenate([_unpack_low(p), _unpack_high(p)], axis=-1).astype(BF16)
    g = jnp.dot(x, wgb_ref[...], preferred_element_type=F32)
    u = jnp.dot(x, wub_ref[...], preferred_element_type=F32)
    h_ref[0] = (_silu(g) * u).astype(BF16)


def _ffn_up(xg_packed, w_gate, w_up, layer):
    e = xg_packed.shape[0]
    me = xg_packed.shape[1] // PACK_ROWS
    d, f = w_gate.shape[-2:]
    tm, tn = me // 4, 512
    wspec = pl.BlockSpec((1, 1, d, tn), lambda ei, j, i: (layer, ei, 0, j))
    return pl.pallas_call(
        _ffn_up_kernel,
        out_shape=jax.ShapeDtypeStruct((e, me, f), BF16),
        grid=(e, f // tn, me // tm),
        in_specs=[pl.BlockSpec((1, tm * PACK_ROWS, LANES), lambda ei, j, i: (ei, i, 0)), wspec, wspec],
        out_specs=pl.BlockSpec((1, tm, tn), lambda ei, j, i: (ei, i, j)),
        scratch_shapes=[pltpu.VMEM((d, tn), BF16), pltpu.VMEM((d, tn), BF16)],
        compiler_params=_params(("parallel", "parallel", "arbitrary")),
        name="ffn_up",
    )(xg_packed, w_gate, w_up)


def _ffn_down_kernel(h_ref, w_ref, y_ref, wb_ref):
    @pl.when(pl.program_id(2) == 0)
    def _():
        wb_ref[...] = w_ref[0, 0].astype(BF16)

    y_ref[0] = jnp.dot(h_ref[0], wb_ref[...], preferred_element_type=F32)


def _ffn_down(hid, w_down, layer):
    e, me, f = hid.shape
    d = w_down.shape[-1]
    tm, tn = me // 8, 512
    return pl.pallas_call(
        _ffn_down_kernel,
        out_shape=jax.ShapeDtypeStruct((e, me, d), F32),
        grid=(e, d // tn, me // tm),
        in_specs=[pl.BlockSpec((1, tm, f), lambda ei, j, i: (ei, i, 0)),
                  pl.BlockSpec((1, 1, f, tn), lambda ei, j, i: (layer, ei, 0, j))],
        out_specs=pl.BlockSpec((1, tm, tn), lambda ei, j, i: (ei, i, j)),
        scratch_shapes=[pltpu.VMEM((f, tn), BF16)],
        compiler_params=_params(("parallel", "parallel", "arbitrary")),
        name="ffn_down",
    )(hid, w_down)


def _pack_y_kernel(y_ref, g_ref, o_ref):
    _store_token_rows(o_ref, _pack_bf16_pairs(y_ref[0] * g_ref[0]))


def _pack_y(y, gates):
    e, me, d = y.shape
    tm = me // 8
    return pl.pallas_call(
        _pack_y_kernel,
        out_shape=jax.ShapeDtypeStruct((e * me * PACK_ROWS, LANES), jnp.uint32),
        grid=(e, me // tm),
        in_specs=[pl.BlockSpec((1, tm, d), lambda ei, i: (ei, i, 0)),
                  pl.BlockSpec((1, tm, 1), lambda ei, i: (ei, i, 0))],
        out_specs=pl.BlockSpec((tm * PACK_ROWS, LANES), lambda ei, i: (ei * (me // tm) + i, 0)),
        compiler_params=_params(("parallel", "parallel")),
        name="pack_y",
    )(y, gates.reshape(e, me, 1))


CUMSUM_CHUNK = 256


def _exclusive_cumsum(x, tri):
    r, sl = x.shape
    carry = jnp.zeros((r, 1), F32)
    out = []
    for k in range(sl // CUMSUM_CHUNK):
        xk = x[:, k * CUMSUM_CHUNK:(k + 1) * CUMSUM_CHUNK]
        inc = jnp.dot(xk.astype(BF16), tri, preferred_element_type=F32) + carry
        out.append(inc - xk)
        carry = inc[:, CUMSUM_CHUNK - 1:CUMSUM_CHUNK]
    return jnp.concatenate(out, axis=-1)


def _topk_kernel(cap, a_ref, idx_ref, gate_ref, pos_ref):
    a = a_ref[...]
    r, sl = a.shape
    bits = lax.bitcast_convert_type(a, jnp.int32)

    thr = jnp.zeros((r, 1), jnp.int32)
    for bit in range(30, -1, -1):
        cand = thr | (1 << bit)
        cnt = jnp.sum(jnp.where(bits >= cand, 1, 0), axis=1, keepdims=True)
        thr = jnp.where(cnt >= cap, cand, thr)

    ci = lax.broadcasted_iota(jnp.int32, (CUMSUM_CHUNK, CUMSUM_CHUNK), 0)
    cj = lax.broadcasted_iota(jnp.int32, (CUMSUM_CHUNK, CUMSUM_CHUNK), 1)
    tri = jnp.where(ci <= cj, 1.0, 0.0).astype(BF16)

    gt = bits > thr
    eq = jnp.where(bits == thr, 1.0, 0.0)
    need = (cap - jnp.sum(jnp.where(gt, 1, 0), axis=1, keepdims=True)).astype(F32)
    eq_rank = _exclusive_cumsum(eq, tri)
    sel = jnp.where(gt, 1.0, jnp.where(eq_rank < need, eq, 0.0))
    pos = _exclusive_cumsum(sel, tri)
    pos_ref[...] = pos.astype(jnp.int32)

    pos_sel = jnp.where(sel > 0.0, pos, -1.0)
    tok = lax.broadcasted_iota(jnp.int32, (r, sl), 1).astype(F32)
    capp = idx_ref.shape[1]
    slot = lax.broadcasted_iota(jnp.int32, (r, capp), 1)
    idx_ref[...] = jnp.zeros((r, capp), jnp.int32)
    gate_ref[...] = jnp.zeros((r, capp), F32)

    def compact(s, carry):
        m = pos_sel == jnp.asarray(s, F32)
        t_s = jnp.sum(jnp.where(m, tok, 0.0), axis=1, keepdims=True)
        g_s = jnp.sum(jnp.where(m, a, 0.0), axis=1, keepdims=True)
        idx_ref[...] = jnp.where(slot == s, t_s.astype(jnp.int32), idx_ref[...])
        gate_ref[...] = jnp.where(slot == s, g_s, gate_ref[...])
        return carry

    lax.fori_loop(0, cap, compact, 0)


def _select(aff, cap, r_blk, n_steps, sl, col_blk_of):
    capp = max(cap, LANES)
    return pl.pallas_call(
        functools.partial(_topk_kernel, cap),
        out_shape=(jax.ShapeDtypeStruct((n_steps * r_blk, capp), jnp.int32),
                   jax.ShapeDtypeStruct((n_steps * r_blk, capp), F32),
                   jax.ShapeDtypeStruct((n_steps * r_blk, sl), jnp.int32)),
        grid=(n_steps,),
        in_specs=[pl.BlockSpec((r_blk, sl), col_blk_of)],
        out_specs=(pl.BlockSpec((r_blk, capp), lambda i: (i, 0)),
                   pl.BlockSpec((r_blk, capp), lambda i: (i, 0)),
                   pl.BlockSpec((r_blk, sl), lambda i: (i, 0))),
        compiler_params=_params(("parallel",)),
        name="expert_select",
    )(aff)


GATHER_UNROLL = 8


def _dispatch_kernel(cb, idx_ref, x_ref, o_ref):
    def body(g, carry):
        for k in range(GATHER_UNROLL):
            s = g * GATHER_UNROLL + k
            t = idx_ref[0, 0, s]
            o_ref[pl.ds(pl.multiple_of(s * PACK_ROWS, PACK_ROWS), PACK_ROWS), :] = (
                x_ref[pl.ds(pl.multiple_of(t * PACK_ROWS, PACK_ROWS), PACK_ROWS), :])
        return carry

    lax.fori_loop(0, cb // GATHER_UNROLL, body, 0)


def _dispatch(h2_packed, idx, n_blocks, blk_len, cb):
    e = N_EXPERTS
    out = pl.pallas_call(
        functools.partial(_dispatch_kernel, cb),
        out_shape=jax.ShapeDtypeStruct((e * n_blocks * cb * PACK_ROWS, LANES), jnp.uint32),
        grid=(n_blocks, e),
        in_specs=[pl.BlockSpec((1, 1, cb), lambda b, ei: (b * e + ei, 0, 0), memory_space=pltpu.SMEM),
                  pl.BlockSpec((blk_len * PACK_ROWS, LANES), lambda b, ei: (b, 0))],
        out_specs=pl.BlockSpec((cb * PACK_ROWS, LANES), lambda b, ei: (ei * n_blocks + b, 0)),
        compiler_params=_params(("parallel", "arbitrary")),
        name="dispatch",
    )(idx, h2_packed)
    return out.reshape(e, n_blocks * cb * PACK_ROWS, LANES)


COMBINE_WINDOW = 64


def _combine_kernel(final, cb, tiles_per_block, me, s0_ref, n_ref, idx_ref, yp_ref, x1_ref, g2_ref, fg_ref,
                    o_ref, ybuf, acc, sem):
    e_n = N_EXPERTS
    w = COMBINE_WINDOW
    t = SEQ_TILE
    i = pl.program_id(0)
    b = i // tiles_per_block
    tile_base = (i % tiles_per_block) * t
    acc[...] = jnp.zeros(acc.shape, F32)

    def window(e, start):
        ws = jnp.minimum(start, cb - w)
        row0 = pl.multiple_of((e * me + b * cb + ws) * PACK_ROWS, PACK_ROWS)
        return ws, pltpu.make_async_copy(yp_ref.at[pl.ds(row0, w * PACK_ROWS), :], ybuf.at[e], sem.at[e])

    for e in range(e_n):
        window(e, s0_ref[i * e_n + e])[1].start()

    for e in range(e_n):
        s0 = s0_ref[i * e_n + e]
        end = s0 + n_ref[i * e_n + e]

        def chunk(start, prefetched, e=e, end=end):
            ws, cp = window(e, start)
            if not prefetched:
                cp.start()
            cp.wait()
            cnt = jnp.minimum(end, ws + w) - start

            def row(r, carry):
                slot = start + r
                tok = idx_ref[e, 0, slot] - tile_base
                y = ybuf[e, pl.ds(pl.multiple_of((slot - ws) * PACK_ROWS, PACK_ROWS), PACK_ROWS), :]
                base = pl.multiple_of(tok * CHUNK_ROWS, CHUNK_ROWS)
                acc[pl.ds(base, PACK_ROWS), :] += _unpack_low(y)
                acc[pl.ds(base + PACK_ROWS, PACK_ROWS), :] += _unpack_high(y)
                return carry

            lax.fori_loop(0, cnt, row, 0)
            return start + cnt

        nxt = chunk(s0, True)
        lax.while_loop(lambda st, end=end: st < end, lambda st: chunk(st, False), nxt)

    moe = _load_token_rows(acc, t, CHUNK_ROWS)
    x2 = x1_ref[...] + g2_ref[0] * moe
    if final:
        x2 = x2 * lax.rsqrt(jnp.mean(x2 * x2, axis=-1, keepdims=True) + EPS) * fg_ref[...]
    o_ref[...] = x2


def _combine(y_packed, idx, s0, cnt, x1, mod3, layer, final_g, final, m_ctx, lat_len, cb, me):
    m, d = x1.shape
    t = SEQ_TILE
    tiles_per_block = lat_len // t
    row = _mod_row(m_ctx, lat_len, t)
    base = layer * MOD_ROWS * N_MOD
    grid_spec = pltpu.PrefetchScalarGridSpec(
        num_scalar_prefetch=2,
        grid=(m // t,),
        in_specs=[
            pl.BlockSpec((N_EXPERTS, 1, cb), lambda i, s0r, nr: (i // tiles_per_block, 0, 0),
                         memory_space=pltpu.SMEM),
            pl.BlockSpec(memory_space=pl.ANY),
            pl.BlockSpec((t, d), lambda i, s0r, nr: (i, 0)),
            pl.BlockSpec((1, 1, d), lambda i, s0r, nr: (base + row(i) * N_MOD + 5, 0, 0)),
            pl.BlockSpec((1, d), lambda i, s0r, nr: (0, 0)),
        ],
        out_specs=pl.BlockSpec((t, d), lambda i, s0r, nr: (i, 0)),
        scratch_shapes=[pltpu.VMEM((N_EXPERTS, COMBINE_WINDOW * PACK_ROWS, LANES), jnp.uint32),
                        pltpu.VMEM((t * CHUNK_ROWS, LANES), F32),
                        pltpu.SemaphoreType.DMA((N_EXPERTS,))],
    )
    return pl.pallas_call(
        functools.partial(_combine_kernel, final, cb, tiles_per_block, me),
        out_shape=jax.ShapeDtypeStruct((m, d), F32),
        grid_spec=grid_spec,
        compiler_params=_params(("arbitrary",)),
        name="combine",
    )(s0, cnt, idx, y_packed, x1, mod3, final_g.reshape(1, d))


def _route(aff_t, n_ctx, ctx_len, n_lat, lat_len):
    e = N_EXPERTS
    m_ctx = n_ctx * ctx_len
    cap_c = EC_FACTOR * ctx_len // e
    cb = EC_FACTOR * lat_len // e
    spb = lat_len // ctx_len
    n_cblk = n_ctx // spb
    tpb = lat_len // SEQ_TILE

    a_ctx = aff_t[:, :m_ctx].reshape(e * n_ctx, ctx_len)
    idx_c, gate_c, _ = _select(a_ctx, cap_c, e * n_ctx, 1, ctx_len, lambda i: (0, 0))
    idx_c = idx_c[:, :cap_c].reshape(e, n_cblk, spb, cap_c) + (jnp.arange(spb) * ctx_len)[None, None, :, None]
    idx_c = jnp.swapaxes(idx_c.reshape(e, n_cblk, cb), 0, 1)
    gate_c = gate_c[:, :cap_c].reshape(e, n_cblk * cb)
    s0_c = jnp.broadcast_to((jnp.arange(spb) * cap_c)[None, :, None], (n_cblk, spb, e))
    cnt_c = jnp.full((n_cblk, spb, e), cap_c, jnp.int32)

    blk0 = m_ctx // lat_len
    idx_l, gate_l, pos_l = _select(aff_t, cb, e, n_lat, lat_len, lambda i: (0, blk0 + i))
    idx_l = idx_l[:, :cb].reshape(n_lat, e, cb)
    gate_l = jnp.swapaxes(gate_l[:, :cb].reshape(n_lat, e, cb), 0, 1).reshape(e, n_lat * cb)
    s0_l = jnp.swapaxes(pos_l[:, ::SEQ_TILE].reshape(n_lat, e, tpb), 1, 2)
    nxt = jnp.concatenate([s0_l[:, 1:], jnp.full((n_lat, 1, e), cb, jnp.int32)], axis=1)
    cnt_l = nxt - s0_l

    idx = jnp.concatenate([idx_c, idx_l], axis=0).reshape((n_cblk + n_lat) * e, 1, cb).astype(jnp.int32)
    gates = jnp.concatenate([gate_c, gate_l], axis=1)
    s0 = jnp.concatenate([s0_c.reshape(-1), s0_l.reshape(-1)]).astype(jnp.int32)
    cnt = jnp.concatenate([cnt_c.reshape(-1), cnt_l.reshape(-1)]).astype(jnp.int32)
    return idx, gates, s0, cnt


def kernel(x_prompt, x_sample, state_rglru, state_ret, c, c_ctx, norm1_g, norm2_g, w_mod, b_mod, w_in, conv_w, conv_b, conv_ln_g, conv_ln_b, rg_conv_w, rg_conv_b, rg_w_a, rg_b_a, rg_w_x, rg_b_x, rg_lambda, ret_decay, ret_gn_g, w_out, router_w, exp_w_gate, exp_w_up, exp_w_down, final_g):
    depth = w_in.shape[0]
    n_ctx, ctx_len, d = x_prompt.shape
    n_lat, lat_len, _ = x_sample.shape
    m_ctx, m_lat = n_ctx * ctx_len, n_lat * lat_len
    assert ctx_len == SEQ_TILE and lat_len % SEQ_TILE == 0 and m_ctx % lat_len == 0
    n_ctx_tiles = m_ctx // SEQ_TILE
    tiles_per_lat = lat_len // SEQ_TILE
    n_seq = n_ctx + n_lat
    n_blocks = (m_ctx + m_lat) // lat_len
    cb = EC_FACTOR * lat_len // N_EXPERTS

    x = jnp.concatenate([x_prompt.reshape(m_ctx, d), x_sample.reshape(m_lat, d)], axis=0)

    cond = jnp.zeros((MOD_ROWS, d), F32).at[0].set(c_ctx).at[1:1 + n_lat].set(c)
    mod = _modulation(cond, w_mod, b_mod)
    mod3 = mod.reshape(depth * MOD_ROWS * N_MOD, 1, d)

    tables = _rope_tables(lat_len)
    log_g = jax.nn.log_sigmoid(ret_decay.astype(F32))
    log_g3 = jnp.broadcast_to(
        jnp.swapaxes(log_g, 1, 2).reshape(depth * N_RET_HEADS * 2, 1, 1), (depth * N_RET_HEADS * 2, 1, RET_HEAD_DIM))
    zero_ret = jnp.zeros((n_ctx, 2, N_RET_HEADS, RET_HEAD_DIM, RET_HEAD_DIM), F32)

    rg_states, ret_states = [], []
    for l in range(depth):
        proj = _in_proj(x, norm1_g[l], mod3, l, w_in[l].astype(BF16), m_ctx, lat_len)

        u = _conv_mixer(proj, conv_w[l], conv_b[l], conv_ln_g[l], conv_ln_b[l], n_ctx_tiles, tiles_per_lat)

        h0 = jnp.concatenate([jnp.zeros((n_ctx, 2, D_RG), F32), state_rglru[:, l]], axis=0)
        wg = [jnp.concatenate([_block_diag(rg_w_a[l, dd]), _block_diag(rg_w_x[l, dd])], axis=1).astype(BF16)
              for dd in range(2)]
        bg = [jnp.concatenate([rg_b_a[l, dd], rg_b_x[l, dd]]) for dd in range(2)]
        hf, st_f = _rglru_dir(proj, None, rg_conv_w[l], rg_conv_b[l], wg[0], bg[0], rg_lambda[l, 0],
                              h0[:, 0].reshape(n_seq, 1, D_RG), False, n_ctx_tiles, tiles_per_lat)
        r_out, st_b = _rglru_dir(proj, hf, rg_conv_w[l], rg_conv_b[l], wg[1], bg[1], rg_lambda[l, 1],
                                 h0[:, 1].reshape(n_seq, 1, D_RG), True, n_ctx_tiles, tiles_per_lat)
        rg_states.append(jnp.stack([st_f[:n_ctx, 0], st_b[:n_ctx, 0]], axis=1))

        lg_l = log_g3[l * N_RET_HEADS * 2:(l + 1) * N_RET_HEADS * 2]
        o_ctx, s_ctx = _retention(proj, lg_l, ret_gn_g[l], zero_ret, None, 0, n_ctx, ctx_len, 0)
        o_lat, _ = _retention(proj, lg_l, ret_gn_g[l], state_ret[:, l], tables, m_ctx // lat_len, n_lat, lat_len, 0)
        ret_states.append(s_ctx)
        o = jnp.concatenate([o_ctx, o_lat], axis=0)

        x1, h2p, aff_t = _out_proj(u, r_out, o, w_out[l].astype(BF16), x, norm2_g[l], mod3, l,
                                   router_w[l].T.astype(BF16), m_ctx, lat_len)

        idx, gates, s0, cnt = _route(aff_t, n_ctx, ctx_len, n_lat, lat_len)
        xg = _dispatch(h2p, idx, n_blocks, lat_len, cb)
        hid = _ffn_up(xg, exp_w_gate, exp_w_up, l)
        y = _ffn_down(hid, exp_w_down, l)
        yp = _pack_y(y, gates)
        x = _combine(yp, idx, s0, cnt, x1, mod3, l, final_g, l == depth - 1, m_ctx, lat_len, cb, n_blocks * cb)

    y_prompt = x[:m_ctx].reshape(n_ctx, ctx_len, d)
    y_sample = x[m_ctx:].reshape(n_lat, lat_len, d)
    new_state_rglru = jnp.stack(rg_states, axis=1)
    new_state_ret = jnp.stack(ret_states, axis=1)
    return (y_prompt, y_sample, new_state_rglru, new_state_ret)
```

```python
import functools

import jax
import jax.numpy as jnp
from jax import lax
from jax.experimental import pallas as pl
from jax.experimental.pallas import tpu as pltpu

F32 = jnp.float32
BF16 = jnp.bfloat16

D_MODEL = 2048
D_CONV = 512
CONV_WIDTH = 31
D_RG = 512
RG_BLOCKS = 8
RG_CONV_WIDTH = 4
RG_C = 8.0
N_RET_HEADS = 8
RET_HEAD_DIM = 128
D_RET = N_RET_HEADS * RET_HEAD_DIM
RET_CHUNK = 128
ROPE_BASE = 10000.0
D_IN = 2 * D_CONV + 2 * D_RG + 4 * D_RET
N_EXPERTS = 16
EC_FACTOR = 2
N_MOD = 6
EPS = 1e-6
GRID_W = 64

V7X_VMEM_LIMIT_BYTES = 56 * 1024 * 1024
SEQ_TILE = 256
MOD_ROWS = 16


def _params(sem, vmem=V7X_VMEM_LIMIT_BYTES):
    return pltpu.CompilerParams(dimension_semantics=sem, vmem_limit_bytes=vmem)


def _silu(x):
    return x * jax.nn.sigmoid(x)


def _mod_kernel(c_ref, w_ref, b_ref, o_ref):
    c = _silu(c_ref[...]).astype(BF16)
    o_ref[0] = jnp.dot(c, w_ref[0].astype(BF16), preferred_element_type=F32) + b_ref[0]


def _modulation(cond, w_mod, b_mod):
    depth, d, n = w_mod.shape
    tn = 1024
    return pl.pallas_call(
        _mod_kernel,
        out_shape=jax.ShapeDtypeStruct((depth, MOD_ROWS, n), F32),
        grid=(depth, n // tn),
        in_specs=[
            pl.BlockSpec((MOD_ROWS, d), lambda l, j: (0, 0)),
            pl.BlockSpec((1, d, tn), lambda l, j: (l, 0, j)),
            pl.BlockSpec((1, 1, tn), lambda l, j: (l, 0, j)),
        ],
        out_specs=pl.BlockSpec((1, MOD_ROWS, tn), lambda l, j: (l, 0, j)),
        compiler_params=_params(("parallel", "parallel")),
        name="modulation",
    )(cond, w_mod, b_mod.reshape(depth, 1, n))


def _mod_row(m_ctx, seq_rows, tm):
    def f(i):
        start = i * tm
        return jnp.where(start < m_ctx, 0, 1 + (start - m_ctx) // seq_rows)
    return f


def _split_specs(x, m_ctx, tm, width, n_grid_axes):
    n_c = m_ctx // tm
    if not isinstance(x, tuple):
        if n_grid_axes == 1:
            return [x], [pl.BlockSpec((tm, width), lambda i: (i, 0))]
        return [x], [pl.BlockSpec((tm, width), lambda i, j: (i, 0))]
    xc, xl = x
    lat_blk = lambda i: jnp.maximum(i - n_c, 0)
    if n_grid_axes == 1:
        specs = [pl.BlockSpec((tm, width), lambda i: (jnp.minimum(i, n_c - 1), 0)),
                 pl.BlockSpec((tm, width), lambda i: (lat_blk(i), 0))]
    else:
        specs = [pl.BlockSpec((tm, width), lambda i, j: (jnp.minimum(i, n_c - 1), 0)),
                 pl.BlockSpec((tm, width), lambda i, j: (lat_blk(i), 0))]
    return [xc, xl], specs


def _pick(is_ctx, refs):
    if len(refs) == 1:
        return refs[0][...]
    return jnp.where(is_ctx, refs[0][...], refs[1][...])


def _in_proj_kernel(n_ctx_blk, n_x, *refs):
    x_refs = refs[:n_x]
    g_ref, sc_ref, sh_ref, w_ref, o_ref, h_ref = refs[n_x:]

    @pl.when(pl.program_id(1) == 0)
    def _():
        x = _pick(pl.program_id(0) < n_ctx_blk, x_refs)
        y = x * lax.rsqrt(jnp.mean(x * x, axis=-1, keepdims=True) + EPS) * g_ref[...]
        h_ref[...] = (y * (1.0 + sc_ref[0]) + sh_ref[0]).astype(BF16)

    o_ref[...] = jnp.dot(h_ref[...], w_ref[...], preferred_element_type=F32)


def _in_proj(x, norm_g, mod3, layer, w_in_bf, m_ctx, lat_len, m):
    d, n = w_in_bf.shape
    tm, tn = (512 if isinstance(x, tuple) else 1024), 1024
    row = _mod_row(m_ctx, lat_len, tm)
    base = layer * MOD_ROWS * N_MOD
    xs, xspecs = _split_specs(x, m_ctx, tm, d, 2)
    return pl.pallas_call(
        functools.partial(_in_proj_kernel, m_ctx // tm, len(xs)),
        out_shape=jax.ShapeDtypeStruct((m, n), F32),
        grid=(m // tm, n // tn),
        in_specs=xspecs + [
            pl.BlockSpec((1, d), lambda i, j: (0, 0)),
            pl.BlockSpec((1, 1, d), lambda i, j: (base + row(i) * N_MOD + 1, 0, 0)),
            pl.BlockSpec((1, 1, d), lambda i, j: (base + row(i) * N_MOD + 0, 0, 0)),
            pl.BlockSpec((d, tn), lambda i, j: (0, j)),
        ],
        out_specs=pl.BlockSpec((tm, tn), lambda i, j: (i, j)),
        scratch_shapes=[pltpu.VMEM((tm, d), BF16)],
        compiler_params=_params(("parallel", "arbitrary")),
        name="in_proj",
    )(*xs, norm_g.reshape(1, d), mod3, mod3, w_in_bf)


def _tile_flags(i, n_ctx_tiles, tiles_per_lat):
    j = jnp.maximum(i - n_ctx_tiles, 0) % tiles_per_lat
    is_ctx = i < n_ctx_tiles
    first = jnp.logical_or(is_ctx, j == 0)
    last = jnp.logical_or(is_ctx, j == tiles_per_lat - 1)
    return first, last


CONV_HALO = 16


def _conv_kernel(n_ctx_tiles, tiles_per_lat,
                 ca_ref, cb_ref, pa_ref, pb_ref, na_ref, nb_ref, w_ref, b_ref, lg_ref, lb_ref,
                 o_ref, u_ref, sh_ref):
    i = pl.program_id(0)
    first, last = _tile_flags(i, n_ctx_tiles, tiles_per_lat)
    t = SEQ_TILE
    hl = CONV_HALO
    u_ref[pl.ds(hl, t), :] = ca_ref[...] * jax.nn.sigmoid(cb_ref[...])
    prev = pa_ref[...] * jax.nn.sigmoid(pb_ref[...])
    nxt = na_ref[...] * jax.nn.sigmoid(nb_ref[...])
    u_ref[pl.ds(0, hl), :] = jnp.where(first, 0.0, prev)
    u_ref[pl.ds(hl + t, hl), :] = jnp.where(last, 0.0, nxt)

    sub = 8
    rows = t + 2 * hl - sub
    for s in range(1, sub):
        sh_ref[s - 1, pl.ds(0, rows), :] = u_ref[pl.ds(s, rows), :]

    pad = CONV_WIDTH // 2
    lane = 128
    cols = []
    for c in range(D_CONV // lane):
        acc = jnp.zeros((t, lane), F32)
        for k in range(CONV_WIDTH):
            off = hl - pad + k
            s, base = off % sub, off - off % sub
            src = u_ref if s == 0 else sh_ref.at[s - 1]
            acc = acc + src[pl.ds(base, t), pl.ds(c * lane, lane)] * w_ref[pl.ds(k, 1), pl.ds(c * lane, lane)]
        cols.append(acc)
    y = jnp.concatenate(cols, axis=-1) + b_ref[...]
    mu = jnp.mean(y, axis=-1, keepdims=True)
    var = jnp.mean(jnp.square(y - mu), axis=-1, keepdims=True)
    z = (y - mu) * lax.rsqrt(var + EPS) * lg_ref[...] + lb_ref[...]
    o_ref[...] = _silu(z).astype(BF16)


def _conv_mixer(proj, conv_w, conv_b, ln_g, ln_b, n_ctx_tiles, tiles_per_lat):
    m = proj.shape[0]
    t, hl = SEQ_TILE, CONV_HALO
    r = t // hl
    n_tiles = m // t
    last_halo = m // hl - 1
    w_pad = jnp.zeros((32, D_CONV), F32).at[:CONV_WIDTH].set(conv_w)
    row = lambda v: v.reshape(1, D_CONV)
    cur = lambda c: pl.BlockSpec((t, D_CONV), lambda i: (i, c))
    prv = lambda c: pl.BlockSpec((hl, D_CONV), lambda i: (jnp.maximum(i * r - 1, 0), c))
    nxt = lambda c: pl.BlockSpec((hl, D_CONV), lambda i: (jnp.minimum((i + 1) * r, last_halo), c))
    vec = pl.BlockSpec((1, D_CONV), lambda i: (0, 0))
    return pl.pallas_call(
        functools.partial(_conv_kernel, n_ctx_tiles, tiles_per_lat),
        out_shape=jax.ShapeDtypeStruct((m, D_CONV), BF16),
        grid=(n_tiles,),
        in_specs=[cur(0), cur(1), prv(0), prv(1), nxt(0), nxt(1),
                  pl.BlockSpec((32, D_CONV), lambda i: (0, 0)), vec, vec, vec],
        out_specs=pl.BlockSpec((t, D_CONV), lambda i: (i, 0)),
        scratch_shapes=[pltpu.VMEM((t + 2 * hl, D_CONV), F32), pltpu.VMEM((7, t + 2 * hl, D_CONV), F32)],
        compiler_params=_params(("parallel",)),
        name="conv_mixer",
    )(proj, proj, proj, proj, proj, proj, w_pad, row(conv_b), row(ln_g), row(ln_b))


RG_HALO = 8


def _shift_rows(x, s, fill, reverse, row):
    t = x.shape[0]
    if s % 8 == 0:
        pad = jnp.full((s, x.shape[1]), fill, x.dtype)
        return jnp.concatenate([x[s:], pad] if reverse else [pad, x[:t - s]], axis=0)
    if reverse:
        return jnp.where(row < t - s, pltpu.roll(x, t - s, axis=0), fill)
    return jnp.where(row >= s, pltpu.roll(x, s, axis=0), fill)


def _rglru_kernel(reverse, with_out, n_ctx_tiles, tiles_per_lat, *refs):
    if with_out:
        (x_ref, xp_ref, xn_ref, cw_ref, cb_ref, wg_ref, bg_ref, lam_ref, h0_ref, gate_ref, hf_ref,
         o_ref, st_ref, carry_ref, ext_ref) = refs
    else:
        (x_ref, xp_ref, xn_ref, cw_ref, cb_ref, wg_ref, bg_ref, lam_ref, h0_ref,
         o_ref, st_ref, carry_ref, ext_ref) = refs
    n_tiles = pl.num_programs(0)
    step = pl.program_id(0)
    i = (n_tiles - 1 - step) if reverse else step
    first, last = _tile_flags(i, n_ctx_tiles, tiles_per_lat)
    t = SEQ_TILE

    ext_ref[pl.ds(0, RG_HALO), :] = jnp.where(first, 0.0, xp_ref[...])
    ext_ref[pl.ds(RG_HALO, t), :] = x_ref[...]
    ext_ref[pl.ds(RG_HALO + t, RG_HALO), :] = jnp.where(last, 0.0, xn_ref[...])
    xr = cb_ref[...] + jnp.zeros((t, D_RG), F32)
    for k in range(RG_CONV_WIDTH):
        off = RG_HALO - RG_CONV_WIDTH // 2 + k
        xr = xr + ext_ref[pl.ds(off, t), :] * cw_ref[pl.ds(k, 1), :]

    pre = jnp.dot(xr.astype(BF16), wg_ref[...], preferred_element_type=F32) + bg_ref[...]
    r = jax.nn.sigmoid(pre[:, :D_RG])
    ig = jax.nn.sigmoid(pre[:, D_RG:])
    nl = -lam_ref[...]
    softplus = jnp.maximum(nl, 0.0) + jnp.log1p(jnp.exp(-jnp.abs(nl)))
    log_a = -RG_C * r * softplus
    a = jnp.exp(log_a)
    u = jnp.sqrt(-jnp.tanh(log_a) * (a * a + 1.0)) * (ig * xr)

    row = lax.broadcasted_iota(jnp.int32, (t, D_RG), 0)
    s = 1
    while s < t:
        a_s = _shift_rows(a, s, 1.0, reverse, row)
        u_s = _shift_rows(u, s, 0.0, reverse, row)
        u = a * u_s + u
        a = a * a_s
        s *= 2

    seq_start = last if reverse else first
    h_in = jnp.where(seq_start, h0_ref[0], carry_ref[...])
    h = a * h_in + u
    end = h[0:1] if reverse else h[t - 1:t]
    carry_ref[...] = end
    st_ref[0] = end
    if with_out:
        o_ref[...] = (jax.nn.gelu(gate_ref[...]) * (hf_ref[...] + h)).astype(BF16)
    else:
        o_ref[...] = h


def _seq_of_tile(n_ctx_tiles, tiles_per_lat):
    def f(i):
        return jnp.where(i < n_ctx_tiles, i, n_ctx_tiles + (i - n_ctx_tiles) // tiles_per_lat)
    return f


def _rglru_dir(proj, hf, conv_w, conv_b, w_gate_bf, b_gate, lam, h0, reverse, n_ctx_tiles, tiles_per_lat):
    m = proj.shape[0]
    t, hl = SEQ_TILE, RG_HALO
    r = t // hl
    n_tiles = m // t
    last_halo = m // hl - 1
    with_out = hf is not None
    tile = (lambda s: n_tiles - 1 - s) if reverse else (lambda s: s)
    seq = _seq_of_tile(n_ctx_tiles, tiles_per_lat)
    col_x = (2 * D_CONV + D_RG) // D_RG
    col_g = (2 * D_CONV) // D_RG
    cw_pad = jnp.zeros((8, D_RG), F32).at[:RG_CONV_WIDTH].set(conv_w)
    in_specs = [
        pl.BlockSpec((t, D_RG), lambda s: (tile(s), col_x)),
        pl.BlockSpec((hl, D_RG), lambda s: (jnp.maximum(tile(s) * r - 1, 0), col_x)),
        pl.BlockSpec((hl, D_RG), lambda s: (jnp.minimum((tile(s) + 1) * r, last_halo), col_x)),
        pl.BlockSpec((8, D_RG), lambda s: (0, 0)),
        pl.BlockSpec((1, D_RG), lambda s: (0, 0)),
        pl.BlockSpec((D_RG, 2 * D_RG), lambda s: (0, 0)),
        pl.BlockSpec((1, 2 * D_RG), lambda s: (0, 0)),
        pl.BlockSpec((1, D_RG), lambda s: (0, 0)),
        pl.BlockSpec((1, 1, D_RG), lambda s: (seq(tile(s)), 0, 0)),
    ]
    args = [proj, proj, proj, cw_pad, conv_b.reshape(1, D_RG), w_gate_bf, b_gate.reshape(1, 2 * D_RG),
            lam.reshape(1, D_RG), h0]
    if with_out:
        in_specs += [pl.BlockSpec((t, D_RG), lambda s: (tile(s), col_g)),
                     pl.BlockSpec((t, D_RG), lambda s: (tile(s), 0))]
        args += [proj, hf]
    out_dtype = BF16 if with_out else F32
    return pl.pallas_call(
        functools.partial(_rglru_kernel, reverse, with_out, n_ctx_tiles, tiles_per_lat),
        out_shape=(jax.ShapeDtypeStruct((m, D_RG), out_dtype),
                   jax.ShapeDtypeStruct((n_tiles, 1, D_RG), F32)),
        grid=(n_tiles,),
        in_specs=in_specs,
        out_specs=(pl.BlockSpec((t, D_RG), lambda s: (tile(s), 0)),
                   pl.BlockSpec((1, 1, D_RG), lambda s: (tile(s), 0, 0))),
        scratch_shapes=[pltpu.VMEM((1, D_RG), F32), pltpu.VMEM((t + 2 * hl, D_RG), F32)],
        compiler_params=_params(("arbitrary",)),
        name="rglru_bwd" if reverse else "rglru_fwd",
    )(*args)


def _block_diag(w):
    g, bi, bj = w.shape
    eye = jnp.eye(g, dtype=w.dtype)
    return (eye[:, None, :, None] * w[:, :, None, :]).reshape(g * bi, g * bj)


RET_GROUP = 4
RET_OUT_GROUP = 8


def _ret_kernel(n_chunks, rotary, q_ref, k_ref, v_ref, g_ref, lg_ref, gn_ref, s0_ref, *rest):
    if rotary:
        cos_ref, sin_ref, o_ref, st_ref, qb_ref, kb_ref, sf_ref, sb_ref, kvf_ref, kvb_ref = rest
    else:
        o_ref, st_ref, qb_ref, kb_ref, sf_ref, sb_ref, kvf_ref, kvb_ref = rest
    c = RET_CHUNK
    dh = RET_HEAD_DIM
    scale = RET_HEAD_DIM ** -0.5

    q = q_ref[...]
    k = k_ref[...] * scale
    if rotary:
        lane = lax.broadcasted_iota(jnp.int32, q.shape, 1)
        low = (lane % (dh // 2)) < (dh // 4)
        cos = cos_ref[...]
        sin = sin_ref[...]

        def rot(x):
            partner = jnp.where(low, pltpu.roll(x, dh - dh // 4, axis=1), pltpu.roll(x, dh // 4, axis=1))
            return x * cos + partner * sin
        q = rot(q)
        k = rot(k)
    qb_ref[...] = q
    kb_ref[...] = k

    lgf = lg_ref[0]
    lgb = lg_ref[1]
    ii = lax.broadcasted_iota(jnp.int32, (c, c), 0).astype(F32)
    jj = lax.broadcasted_iota(jnp.int32, (c, c), 1).astype(F32)
    rel = ii - jj
    decay = (jnp.where(rel >= 0, jnp.exp(jnp.maximum(rel, 0.0) * lgf), 0.0)
             + jnp.where(rel <= 0, jnp.exp(jnp.maximum(-rel, 0.0) * lgb), 0.0))
    qwf = jnp.exp((ii + 1.0) * lgf)
    kwf = jnp.exp((c - 1.0 - ii) * lgf)
    qwb = jnp.exp((c - ii) * lgb)
    kwb = jnp.exp(ii * lgb)
    cdf = jnp.exp(c * lgf)
    cdb = jnp.exp(c * lgb)

    def chunk(ci):
        return pl.ds(pl.multiple_of(ci * c, c), c)

    group = min(RET_GROUP, n_chunks)

    def kv_group(gi, carry):
        cis = [gi * group + j for j in range(group)]
        dims = (((0,), (0,)), ((), ()))
        ops = []
        for ci in cis:
            kc = kb_ref[chunk(ci), :]
            ops.append(((kc * kwf).astype(BF16), (kc * kwb).astype(BF16), v_ref[chunk(ci), :].astype(BF16)))
        for ci, (kf, kb, vc) in zip(cis, ops):
            kvf_ref[ci] = lax.dot_general(kf, vc, dims, preferred_element_type=F32)
            kvb_ref[ci] = lax.dot_general(kb, vc, dims, preferred_element_type=F32)
        return carry

    lax.fori_loop(0, n_chunks // group, kv_group, 0)

    def fwd(ci, s):
        sf_ref[ci] = s.astype(BF16)
        return cdf * s + kvf_ref[ci]

    def bwd(step, s):
        ci = n_chunks - 1 - step
        sb_ref[ci] = s.astype(BF16)
        return cdb * s + kvb_ref[ci]

    st_ref[0, 0, 0] = lax.fori_loop(0, n_chunks, fwd, s0_ref[0, 0, 0])
    st_ref[0, 1, 0] = lax.fori_loop(0, n_chunks, bwd, s0_ref[0, 1, 0])

    gn = gn_ref[...]

    out_grp = min(RET_OUT_GROUP, n_chunks)

    def out_group(gi, carry):
        cis = [gi * out_grp + j for j in range(out_grp)]
        qfs = [qb_ref[chunk(ci), :] for ci in cis]
        scores = [lax.dot_general(qf.astype(BF16), kb_ref[chunk(ci), :].astype(BF16), (((1,), (1,)), ((), ())),
                                  preferred_element_type=F32) for ci, qf in zip(cis, qfs)]
        outs = []
        for ci, qf, sc in zip(cis, qfs, scores):
            lhs = jnp.concatenate([(sc * decay).astype(BF16), (qf * qwf).astype(BF16), (qf * qwb).astype(BF16)],
                                  axis=1)
            rhs = jnp.concatenate([v_ref[chunk(ci), :].astype(BF16), sf_ref[ci], sb_ref[ci]], axis=0)
            outs.append(jnp.dot(lhs, rhs, preferred_element_type=F32))
        for ci, o in zip(cis, outs):
            mu = jnp.mean(o, axis=-1, keepdims=True)
            var = jnp.mean(jnp.square(o - mu), axis=-1, keepdims=True)
            o = (o - mu) * lax.rsqrt(var + EPS) * gn
            o_ref[chunk(ci), :] = (_silu(g_ref[chunk(ci), :]) * o).astype(BF16)
        return carry

    lax.fori_loop(0, n_chunks // out_grp, out_group, 0)


def _retention(proj, log_g3, gn_g, s0, tables, row_blk0, n_seq, length, layer):
    dh, nh = RET_HEAD_DIM, N_RET_HEADS
    rotary = tables is not None
    col0 = (2 * D_CONV + 2 * D_RG) // dh
    blk = lambda part: pl.BlockSpec((length, dh), lambda b, h: (row_blk0 + b, col0 + part * nh + h))
    in_specs = [blk(0), blk(1), blk(2), blk(3),
                pl.BlockSpec((2, 1, dh), lambda b, h: ((layer * nh + h), 0, 0)),
                pl.BlockSpec((1, dh), lambda b, h: (0, h)),
                pl.BlockSpec((1, 2, 1, dh, dh), lambda b, h: (b, 0, h, 0, 0))]
    args = [proj, proj, proj, proj, log_g3, gn_g.reshape(1, D_RET), s0]
    if rotary:
        tab = pl.BlockSpec((length, dh), lambda b, h: (0, 0))
        in_specs += [tab, tab]
        args += list(tables)
    n_chunks = length // RET_CHUNK
    return pl.pallas_call(
        functools.partial(_ret_kernel, n_chunks, rotary),
        out_shape=(jax.ShapeDtypeStruct((n_seq * length, D_RET), BF16),
                   jax.ShapeDtypeStruct((n_seq, 2, nh, dh, dh), F32)),
        grid=(n_seq, nh),
        in_specs=in_specs,
        out_specs=(pl.BlockSpec((length, dh), lambda b, h: (b, h)),
                   pl.BlockSpec((1, 2, 1, dh, dh), lambda b, h: (b, 0, h, 0, 0))),
        scratch_shapes=[pltpu.VMEM((length, dh), F32), pltpu.VMEM((length, dh), F32),
                        pltpu.VMEM((n_chunks, dh, dh), BF16), pltpu.VMEM((n_chunks, dh, dh), BF16),
                        pltpu.VMEM((n_chunks, dh, dh), F32), pltpu.VMEM((n_chunks, dh, dh), F32)],
        compiler_params=_params(("parallel", "parallel")),
        name="retention_lat" if rotary else "retention_ctx",
    )(*args)


def _rope_tables(length):
    dh = RET_HEAD_DIM
    quarter = dh // 4
    pos = jnp.arange(length)
    rows = (pos // GRID_W).astype(F32)
    cols = (pos % GRID_W).astype(F32)
    freq = ROPE_BASE ** (-jnp.arange(quarter, dtype=F32) / quarter)
    ang_r = rows[:, None] * freq[None, :]
    ang_c = cols[:, None] * freq[None, :]
    cos = jnp.concatenate([jnp.cos(ang_r), jnp.cos(ang_r), jnp.cos(ang_c), jnp.cos(ang_c)], axis=-1)
    sin = jnp.concatenate([-jnp.sin(ang_r), jnp.sin(ang_r), -jnp.sin(ang_c), jnp.sin(ang_c)], axis=-1)
    return cos, sin


LANES = 128
PACK_ROWS = D_MODEL // (2 * LANES)
CHUNK_ROWS = D_MODEL // LANES
HIGH_HALF = 0xFFFF0000


def _pack_bf16_pairs(x):
    half = x.shape[1] // 2
    lo = lax.bitcast_convert_type(x[:, :half].astype(BF16).astype(F32), jnp.uint32) >> 16
    hi = lax.bitcast_convert_type(x[:, half:].astype(BF16).astype(F32), jnp.uint32) & jnp.uint32(HIGH_HALF)
    return lo | hi


def _unpack_low(p):
    return lax.bitcast_convert_type(p << 16, F32)


def _unpack_high(p):
    return lax.bitcast_convert_type(p & jnp.uint32(HIGH_HALF), F32)


def _store_token_rows(ref, v):
    tm, n = v.shape[0], v.shape[1] // LANES
    for j in range(n):
        ref[pl.ds(j, tm, stride=n), :] = v[:, j * LANES:(j + 1) * LANES]


def _load_token_rows(ref, tm, n):
    return jnp.concatenate([ref[pl.ds(j, tm, stride=n), :] for j in range(n)], axis=-1)


def _out_proj_kernel(n_ctx_blk, n_x, u_ref, r_ref, oc_ref, ol_ref, w_ref, *refs):
    x_refs = refs[:n_x]
    g1_ref, ng_ref, sc_ref, sh_ref, rw_ref, x1_ref, h2_ref, aff_ref = refs[n_x:]
    is_ctx = pl.program_id(0) < n_ctx_blk
    lhs = jnp.concatenate([u_ref[...], r_ref[...], _pick(is_ctx, (oc_ref, ol_ref))], axis=-1)
    y = jnp.dot(lhs, w_ref[...], preferred_element_type=F32)
    x1 = _pick(is_ctx, x_refs) + g1_ref[0] * y
    x1_ref[...] = x1
    n = x1 * lax.rsqrt(jnp.mean(x1 * x1, axis=-1, keepdims=True) + EPS) * ng_ref[...]
    h2f = n * (1.0 + sc_ref[0]) + sh_ref[0]
    _store_token_rows(h2_ref, _pack_bf16_pairs(h2f))
    h2 = h2f.astype(BF16)
    logits = lax.dot_general(rw_ref[...], h2, (((1,), (1,)), ((), ())), preferred_element_type=F32)
    z = logits - jnp.max(logits, axis=0, keepdims=True)
    e = jnp.exp(z)
    aff_ref[...] = e / jnp.sum(e, axis=0, keepdims=True)


def _out_proj(u, r, o, w_out_bf, x, norm_g, mod3, layer, router_wt_bf, m_ctx, lat_len):
    m, d = u.shape[0], w_out_bf.shape[1]
    tm = 512
    row = _mod_row(m_ctx, lat_len, tm)
    base = layer * MOD_ROWS * N_MOD
    modspec = lambda j: pl.BlockSpec((1, 1, d), lambda i: (base + row(i) * N_MOD + j, 0, 0))
    os_, ospecs = _split_specs(o, m_ctx, tm, D_RET, 1)
    xs, xspecs = _split_specs(x, m_ctx, tm, d, 1)
    return pl.pallas_call(
        functools.partial(_out_proj_kernel, m_ctx // tm, len(xs)),
        out_shape=(jax.ShapeDtypeStruct((m, d), F32),
                   jax.ShapeDtypeStruct((m * PACK_ROWS, LANES), jnp.uint32),
                   jax.ShapeDtypeStruct((N_EXPERTS, m), F32)),
        grid=(m // tm,),
        in_specs=[
            pl.BlockSpec((tm, D_CONV), lambda i: (i, 0)),
            pl.BlockSpec((tm, D_RG), lambda i: (i, 0)),
            *ospecs,
            pl.BlockSpec((d, d), lambda i: (0, 0)),
            *xspecs,
            modspec(2),
            pl.BlockSpec((1, d), lambda i: (0, 0)),
            modspec(4), modspec(3),
            pl.BlockSpec((N_EXPERTS, d), lambda i: (0, 0)),
        ],
        out_specs=(pl.BlockSpec((tm, d), lambda i: (i, 0)),
                   pl.BlockSpec((tm * PACK_ROWS, LANES), lambda i: (i, 0)),
                   pl.BlockSpec((N_EXPERTS, tm), lambda i: (0, i))),
        compiler_params=_params(("parallel",)),
        name="out_proj",
    )(u, r, *os_, w_out_bf, *xs, mod3, norm_g.reshape(1, d), mod3, mod3, router_wt_bf)


def _ffn_up_kernel(x_ref, wg_ref, wu_ref, h_ref, wgb_ref, wub_ref):
    @pl.when(pl.program_id(2) == 0)
    def _():
        wgb_ref[...] = wg_ref[0, 0].astype(BF16)
        wub_ref[...] = wu_ref[0, 0].astype(BF16)

    tm = h_ref.shape[1]
    p = _load_token_rows(x_ref.at[0], tm, PACK_ROWS)
    x = jnp.concatenate([_unpack_low(p), _unpack_high(p)], axis=-1).astype(BF16)
    g = jnp.dot(x, wgb_ref[...], preferred_element_type=F32)
    u = jnp.dot(x, wub_ref[...], preferred_element_type=F32)
    h_ref[0] = (_silu(g) * u).astype(BF16)


def _ffn_up(xg_packed, w_gate, w_up, layer):
    e = xg_packed.shape[0]
    me = xg_packed.shape[1] // PACK_ROWS
    d, f = w_gate.shape[-2:]
    tm, tn = me // 4, 512
    wspec = pl.BlockSpec((1, 1, d, tn), lambda ei, j, i: (layer, ei, 0, j))
    return pl.pallas_call(
        _ffn_up_kernel,
        out_shape=jax.ShapeDtypeStruct((e, me, f), BF16),
        grid=(e, f // tn, me // tm),
        in_specs=[pl.BlockSpec((1, tm * PACK_ROWS, LANES), lambda ei, j, i: (ei, i, 0)), wspec, wspec],
        out_specs=pl.BlockSpec((1, tm, tn), lambda ei, j, i: (ei, i, j)),
        scratch_shapes=[pltpu.VMEM((d, tn), BF16), pltpu.VMEM((d, tn), BF16)],
        compiler_params=_params(("parallel", "parallel", "arbitrary")),
        name="ffn_up",
    )(xg_packed, w_gate, w_up)


def _ffn_down_kernel(h_ref, w_ref, g_ref, y_ref, wb_ref):
    @pl.when(pl.program_id(2) == 0)
    def _():
        wb_ref[...] = w_ref[0, 0].astype(BF16)

    y = jnp.dot(h_ref[0], wb_ref[...], preferred_element_type=F32)
    y_ref[0] = (y * g_ref[0]).astype(BF16)


def _ffn_down(hid, w_down, gates, layer):
    e, me, f = hid.shape
    d = w_down.shape[-1]
    tm, tn = me // 8, 512
    return pl.pallas_call(
        _ffn_down_kernel,
        out_shape=jax.ShapeDtypeStruct((e, me, d), BF16),
        grid=(e, d // tn, me // tm),
        in_specs=[pl.BlockSpec((1, tm, f), lambda ei, j, i: (ei, i, 0)),
                  pl.BlockSpec((1, 1, f, tn), lambda ei, j, i: (layer, ei, 0, j)),
                  pl.BlockSpec((1, tm, 1), lambda ei, j, i: (ei, i, 0))],
        out_specs=pl.BlockSpec((1, tm, tn), lambda ei, j, i: (ei, i, j)),
        scratch_shapes=[pltpu.VMEM((f, tn), BF16)],
        compiler_params=_params(("parallel", "parallel", "arbitrary")),
        name="ffn_down",
    )(hid, w_down, gates.reshape(e, me, 1))


def _pack_y_kernel(y_ref, o_ref):
    _store_token_rows(o_ref, _pack_bf16_pairs(y_ref[0]))


def _pack_y(y):
    e, me, d = y.shape
    tm = me // 8
    return pl.pallas_call(
        _pack_y_kernel,
        out_shape=jax.ShapeDtypeStruct((e * me * PACK_ROWS, LANES), jnp.uint32),
        grid=(e, me // tm),
        in_specs=[pl.BlockSpec((1, tm, d), lambda ei, i: (ei, i, 0))],
        out_specs=pl.BlockSpec((tm * PACK_ROWS, LANES), lambda ei, i: (ei * (me // tm) + i, 0)),
        compiler_params=_params(("parallel", "parallel")),
        name="pack_y",
    )(y)


CUMSUM_CHUNK = 256
SELECT_GROUP = 8


def _exclusive_cumsum(x, tri):
    r, sl = x.shape
    carry = jnp.zeros((r, 1), F32)
    out = []
    for k in range(sl // CUMSUM_CHUNK):
        xk = x[:, k * CUMSUM_CHUNK:(k + 1) * CUMSUM_CHUNK]
        inc = jnp.dot(xk.astype(BF16), tri, preferred_element_type=F32) + carry
        out.append(inc - xk)
        carry = inc[:, CUMSUM_CHUNK - 1:CUMSUM_CHUNK]
    return jnp.concatenate(out, axis=-1)


def _topk_kernel(cap, a_ref, idx_ref, gate_ref, pos_ref):
    a = a_ref[...]
    r, sl = a.shape
    bits = lax.bitcast_convert_type(a, jnp.int32)

    thr = jnp.zeros((r, 1), jnp.int32)
    for bit in range(30, -1, -1):
        cand = thr | (1 << bit)
        cnt = jnp.sum(jnp.where(bits >= cand, 1, 0), axis=1, keepdims=True)
        thr = jnp.where(cnt >= cap, cand, thr)

    ci = lax.broadcasted_iota(jnp.int32, (CUMSUM_CHUNK, CUMSUM_CHUNK), 0)
    cj = lax.broadcasted_iota(jnp.int32, (CUMSUM_CHUNK, CUMSUM_CHUNK), 1)
    tri = jnp.where(ci <= cj, 1.0, 0.0).astype(BF16)

    gt = bits > thr
    eq = jnp.where(bits == thr, 1.0, 0.0)
    need = (cap - jnp.sum(jnp.where(gt, 1, 0), axis=1, keepdims=True)).astype(F32)
    eq_rank = _exclusive_cumsum(eq, tri)
    sel = jnp.where(gt, 1.0, jnp.where(eq_rank < need, eq, 0.0))
    pos = _exclusive_cumsum(sel, tri)
    pos_ref[...] = pos.astype(jnp.int32)

    pos_sel = jnp.where(sel > 0.0, pos, -1.0)
    tok = lax.broadcasted_iota(jnp.int32, (r, sl), 1).astype(F32)
    capp = idx_ref.shape[1]
    slot = lax.broadcasted_iota(jnp.int32, (r, capp), 1)
    idx_ref[...] = jnp.zeros((r, capp), jnp.int32)
    gate_ref[...] = jnp.zeros((r, capp), F32)

    def compact(g, carry):
        s0 = g * SELECT_GROUP
        found = []
        for k in range(SELECT_GROUP):
            m = pos_sel == jnp.asarray(s0 + k, F32)
            found.append((jnp.sum(jnp.where(m, tok, 0.0), axis=1, keepdims=True),
                          jnp.sum(jnp.where(m, a, 0.0), axis=1, keepdims=True)))
        idx_new = idx_ref[...]
        gate_new = gate_ref[...]
        for k, (t_s, g_s) in enumerate(found):
            hit = slot == s0 + k
            idx_new = jnp.where(hit, t_s.astype(jnp.int32), idx_new)
            gate_new = jnp.where(hit, g_s, gate_new)
        idx_ref[...] = idx_new
        gate_ref[...] = gate_new
        return carry

    lax.fori_loop(0, cap // SELECT_GROUP, compact, 0)


def _select(aff, cap, r_blk, n_steps, sl, col_blk_of):
    capp = max(cap, LANES)
    return pl.pallas_call(
        functools.partial(_topk_kernel, cap),
        out_shape=(jax.ShapeDtypeStruct((n_steps * r_blk, capp), jnp.int32),
                   jax.ShapeDtypeStruct((n_steps * r_blk, capp), F32),
                   jax.ShapeDtypeStruct((n_steps * r_blk, sl), jnp.int32)),
        grid=(n_steps,),
        in_specs=[pl.BlockSpec((r_blk, sl), col_blk_of)],
        out_specs=(pl.BlockSpec((r_blk, capp), lambda i: (i, 0)),
                   pl.BlockSpec((r_blk, capp), lambda i: (i, 0)),
                   pl.BlockSpec((r_blk, sl), lambda i: (i, 0))),
        compiler_params=_params(("parallel",)),
        name="expert_select",
    )(aff)


GATHER_UNROLL = 8


def _dispatch_kernel(cb, idx_ref, x_ref, o_ref):
    def body(g, carry):
        for k in range(GATHER_UNROLL):
            s = g * GATHER_UNROLL + k
            t = idx_ref[0, 0, s]
            o_ref[pl.ds(pl.multiple_of(s * PACK_ROWS, PACK_ROWS), PACK_ROWS), :] = (
                x_ref[pl.ds(pl.multiple_of(t * PACK_ROWS, PACK_ROWS), PACK_ROWS), :])
        return carry

    lax.fori_loop(0, cb // GATHER_UNROLL, body, 0)


def _dispatch(h2_packed, idx, n_blocks, blk_len, cb):
    e = N_EXPERTS
    out = pl.pallas_call(
        functools.partial(_dispatch_kernel, cb),
        out_shape=jax.ShapeDtypeStruct((e * n_blocks * cb * PACK_ROWS, LANES), jnp.uint32),
        grid=(n_blocks, e),
        in_specs=[pl.BlockSpec((1, 1, cb), lambda b, ei: (b * e + ei, 0, 0), memory_space=pltpu.SMEM),
                  pl.BlockSpec((blk_len * PACK_ROWS, LANES), lambda b, ei: (b, 0))],
        out_specs=pl.BlockSpec((cb * PACK_ROWS, LANES), lambda b, ei: (ei * n_blocks + b, 0)),
        compiler_params=_params(("parallel", "arbitrary")),
        name="dispatch",
    )(idx, h2_packed)
    return out.reshape(e, n_blocks * cb * PACK_ROWS, LANES)


COMBINE_WINDOW = 64


COMBINE_ROWS = 4


def _combine_kernel(final, cb, tiles_per_block, me, n_ctx_tiles, s0_ref, n_ref, idx_ref, yp_ref, x1_ref, g2_ref,
                    fg_ref, *rest):
    if final:
        oc_ref, ol_ref, ybuf, acc, sem = rest
    else:
        o_ref, ybuf, acc, sem = rest
    e_n = N_EXPERTS
    w = COMBINE_WINDOW
    t = SEQ_TILE
    i = pl.program_id(0)
    n_tiles = pl.num_programs(0)
    par = i % 2
    tile_base = (i % tiles_per_block) * t
    acc[...] = jnp.zeros(acc.shape, F32)

    def window(tile, slot, e, start):
        ws = jnp.minimum(start, cb - w)
        row0 = pl.multiple_of((e * me + (tile // tiles_per_block) * cb + ws) * PACK_ROWS, PACK_ROWS)
        return ws, pltpu.make_async_copy(yp_ref.at[pl.ds(row0, w * PACK_ROWS), :], ybuf.at[slot, e],
                                         sem.at[slot, e])

    def prefetch(tile, slot):
        for e in range(e_n):
            window(tile, slot, e, s0_ref[tile * e_n + e])[1].start()

    @pl.when(i == 0)
    def _():
        prefetch(i, par)

    @pl.when(i + 1 < n_tiles)
    def _():
        prefetch(i + 1, 1 - par)

    for e in range(e_n):
        s0 = s0_ref[i * e_n + e]
        end = s0 + n_ref[i * e_n + e]

        def chunk(start, prefetched, e=e, end=end):
            ws, cp = window(i, par, e, start)
            if not prefetched:
                cp.start()
            cp.wait()
            cnt = jnp.minimum(end, ws + w) - start

            def rows(q, carry):
                loaded = []
                for k in range(COMBINE_ROWS):
                    r = q * COMBINE_ROWS + k
                    slot = start + jnp.minimum(r, cnt - 1)
                    tok = jnp.where(r < cnt, idx_ref[e, 0, slot] - tile_base, t)
                    y = ybuf[par, e, pl.ds(pl.multiple_of((slot - ws) * PACK_ROWS, PACK_ROWS), PACK_ROWS), :]
                    base = pl.multiple_of(tok * CHUNK_ROWS, CHUNK_ROWS)
                    loaded.append((base, acc[pl.ds(base, PACK_ROWS), :] + _unpack_low(y),
                                   acc[pl.ds(base + PACK_ROWS, PACK_ROWS), :] + _unpack_high(y)))
                for base, lo, hi in loaded:
                    acc[pl.ds(base, PACK_ROWS), :] = lo
                    acc[pl.ds(base + PACK_ROWS, PACK_ROWS), :] = hi
                return carry

            lax.fori_loop(0, (cnt + COMBINE_ROWS - 1) // COMBINE_ROWS, rows, 0)
            return start + cnt

        nxt = chunk(s0, True)
        lax.while_loop(lambda st, end=end: st < end, lambda st: chunk(st, False), nxt)

    moe = _load_token_rows(acc, t, CHUNK_ROWS)
    x2 = x1_ref[...] + g2_ref[0] * moe
    if final:
        x2 = x2 * lax.rsqrt(jnp.mean(x2 * x2, axis=-1, keepdims=True) + EPS) * fg_ref[...]

        @pl.when(i < n_ctx_tiles)
        def _():
            oc_ref[...] = x2

        @pl.when(i >= n_ctx_tiles)
        def _():
            ol_ref[...] = x2
    else:
        o_ref[...] = x2


def _combine(y_packed, idx, s0, cnt, x1, mod3, layer, final_g, final, m_ctx, lat_len, cb, me):
    m, d = x1.shape
    t = SEQ_TILE
    tiles_per_block = lat_len // t
    n_ctx_tiles = m_ctx // t
    row = _mod_row(m_ctx, lat_len, t)
    base = layer * MOD_ROWS * N_MOD
    if final:
        out_shape = (jax.ShapeDtypeStruct((m_ctx, d), F32), jax.ShapeDtypeStruct((m - m_ctx, d), F32))
        out_specs = (pl.BlockSpec((t, d), lambda i, s0r, nr: (jnp.minimum(i, n_ctx_tiles - 1), 0)),
                     pl.BlockSpec((t, d), lambda i, s0r, nr: (jnp.maximum(i - n_ctx_tiles, 0), 0)))
    else:
        out_shape = jax.ShapeDtypeStruct((m, d), F32)
        out_specs = pl.BlockSpec((t, d), lambda i, s0r, nr: (i, 0))
    grid_spec = pltpu.PrefetchScalarGridSpec(
        num_scalar_prefetch=2,
        grid=(m // t,),
        in_specs=[
            pl.BlockSpec((N_EXPERTS, 1, cb), lambda i, s0r, nr: (i // tiles_per_block, 0, 0),
                         memory_space=pltpu.SMEM),
            pl.BlockSpec(memory_space=pl.ANY),
            pl.BlockSpec((t, d), lambda i, s0r, nr: (i, 0)),
            pl.BlockSpec((1, 1, d), lambda i, s0r, nr: (base + row(i) * N_MOD + 5, 0, 0)),
            pl.BlockSpec((1, d), lambda i, s0r, nr: (0, 0)),
        ],
        out_specs=out_specs,
        scratch_shapes=[pltpu.VMEM((2, N_EXPERTS, COMBINE_WINDOW * PACK_ROWS, LANES), jnp.uint32),
                        pltpu.VMEM(((t + 1) * CHUNK_ROWS, LANES), F32),
                        pltpu.SemaphoreType.DMA((2, N_EXPERTS))],
    )
    return pl.pallas_call(
        functools.partial(_combine_kernel, final, cb, tiles_per_block, me, n_ctx_tiles),
        out_shape=out_shape,
        grid_spec=grid_spec,
        compiler_params=_params(("arbitrary",)),
        name="combine",
    )(s0, cnt, idx, y_packed, x1, mod3, final_g.reshape(1, d))


def _route(aff_t, n_ctx, ctx_len, n_lat, lat_len):
    e = N_EXPERTS
    m_ctx = n_ctx * ctx_len
    cap_c = EC_FACTOR * ctx_len // e
    cb = EC_FACTOR * lat_len // e
    spb = lat_len // ctx_len
    n_cblk = n_ctx // spb
    tpb = lat_len // SEQ_TILE

    a_ctx = aff_t[:, :m_ctx].reshape(e * n_ctx, ctx_len)
    idx_c, gate_c, _ = _select(a_ctx, cap_c, e * n_ctx, 1, ctx_len, lambda i: (0, 0))
    idx_c = idx_c[:, :cap_c].reshape(e, n_cblk, spb, cap_c) + (jnp.arange(spb) * ctx_len)[None, None, :, None]
    idx_c = jnp.swapaxes(idx_c.reshape(e, n_cblk, cb), 0, 1)
    gate_c = gate_c[:, :cap_c].reshape(e, n_cblk * cb)
    s0_c = jnp.broadcast_to((jnp.arange(spb) * cap_c)[None, :, None], (n_cblk, spb, e))
    cnt_c = jnp.full((n_cblk, spb, e), cap_c, jnp.int32)

    blk0 = m_ctx // lat_len
    idx_l, gate_l, pos_l = _select(aff_t, cb, e, n_lat, lat_len, lambda i: (0, blk0 + i))
    idx_l = idx_l[:, :cb].reshape(n_lat, e, cb)
    gate_l = jnp.swapaxes(gate_l[:, :cb].reshape(n_lat, e, cb), 0, 1).reshape(e, n_lat * cb)
    s0_l = jnp.swapaxes(pos_l[:, ::SEQ_TILE].reshape(n_lat, e, tpb), 1, 2)
    nxt = jnp.concatenate([s0_l[:, 1:], jnp.full((n_lat, 1, e), cb, jnp.int32)], axis=1)
    cnt_l = nxt - s0_l

    idx = jnp.concatenate([idx_c, idx_l], axis=0).reshape((n_cblk + n_lat) * e, 1, cb).astype(jnp.int32)
    gates = jnp.concatenate([gate_c, gate_l], axis=1)
    s0 = jnp.concatenate([s0_c.reshape(-1), s0_l.reshape(-1)]).astype(jnp.int32)
    cnt = jnp.concatenate([cnt_c.reshape(-1), cnt_l.reshape(-1)]).astype(jnp.int32)
    return idx, gates, s0, cnt


def kernel(x_prompt, x_sample, state_rglru, state_ret, c, c_ctx, norm1_g, norm2_g, w_mod, b_mod, w_in, conv_w, conv_b, conv_ln_g, conv_ln_b, rg_conv_w, rg_conv_b, rg_w_a, rg_b_a, rg_w_x, rg_b_x, rg_lambda, ret_decay, ret_gn_g, w_out, router_w, exp_w_gate, exp_w_up, exp_w_down, final_g):
    depth = w_in.shape[0]
    n_ctx, ctx_len, d = x_prompt.shape
    n_lat, lat_len, _ = x_sample.shape
    m_ctx, m_lat = n_ctx * ctx_len, n_lat * lat_len
    assert ctx_len == SEQ_TILE and lat_len % SEQ_TILE == 0 and m_ctx % lat_len == 0
    n_ctx_tiles = m_ctx // SEQ_TILE
    tiles_per_lat = lat_len // SEQ_TILE
    n_seq = n_ctx + n_lat
    n_blocks = (m_ctx + m_lat) // lat_len
    cb = EC_FACTOR * lat_len // N_EXPERTS

    x = (x_prompt.reshape(m_ctx, d), x_sample.reshape(m_lat, d))

    cond = jnp.zeros((MOD_ROWS, d), F32).at[0].set(c_ctx).at[1:1 + n_lat].set(c)
    mod = _modulation(cond, w_mod, b_mod)
    mod3 = mod.reshape(depth * MOD_ROWS * N_MOD, 1, d)

    tables = _rope_tables(lat_len)
    log_g = jax.nn.log_sigmoid(ret_decay.astype(F32))
    log_g3 = jnp.broadcast_to(
        jnp.swapaxes(log_g, 1, 2).reshape(depth * N_RET_HEADS * 2, 1, 1), (depth * N_RET_HEADS * 2, 1, RET_HEAD_DIM))
    zero_ret = jnp.zeros((n_ctx, 2, N_RET_HEADS, RET_HEAD_DIM, RET_HEAD_DIM), F32)

    rg_states, ret_states = [], []
    for l in range(depth):
        proj = _in_proj(x, norm1_g[l], mod3, l, w_in[l].astype(BF16), m_ctx, lat_len, m_ctx + m_lat)

        u = _conv_mixer(proj, conv_w[l], conv_b[l], conv_ln_g[l], conv_ln_b[l], n_ctx_tiles, tiles_per_lat)

        h0 = jnp.concatenate([jnp.zeros((n_ctx, 2, D_RG), F32), state_rglru[:, l]], axis=0)
        wg = [jnp.concatenate([_block_diag(rg_w_a[l, dd]), _block_diag(rg_w_x[l, dd])], axis=1).astype(BF16)
              for dd in range(2)]
        bg = [jnp.concatenate([rg_b_a[l, dd], rg_b_x[l, dd]]) for dd in range(2)]
        hf, st_f = _rglru_dir(proj, None, rg_conv_w[l], rg_conv_b[l], wg[0], bg[0], rg_lambda[l, 0],
                              h0[:, 0].reshape(n_seq, 1, D_RG), False, n_ctx_tiles, tiles_per_lat)
        r_out, st_b = _rglru_dir(proj, hf, rg_conv_w[l], rg_conv_b[l], wg[1], bg[1], rg_lambda[l, 1],
                                 h0[:, 1].reshape(n_seq, 1, D_RG), True, n_ctx_tiles, tiles_per_lat)
        rg_states.append(jnp.stack([st_f[:n_ctx, 0], st_b[:n_ctx, 0]], axis=1))

        lg_l = log_g3[l * N_RET_HEADS * 2:(l + 1) * N_RET_HEADS * 2]
        o_ctx, s_ctx = _retention(proj, lg_l, ret_gn_g[l], zero_ret, None, 0, n_ctx, ctx_len, 0)
        o_lat, _ = _retention(proj, lg_l, ret_gn_g[l], state_ret[:, l], tables, m_ctx // lat_len, n_lat, lat_len, 0)
        ret_states.append(s_ctx)

        x1, h2p, aff_t = _out_proj(u, r_out, (o_ctx, o_lat), w_out[l].astype(BF16), x, norm2_g[l], mod3, l,
                                   router_w[l].T.astype(BF16), m_ctx, lat_len)

        idx, gates, s0, cnt = _route(aff_t, n_ctx, ctx_len, n_lat, lat_len)
        xg = _dispatch(h2p, idx, n_blocks, lat_len, cb)
        hid = _ffn_up(xg, exp_w_gate, exp_w_up, l)
        yp = _pack_y(_ffn_down(hid, exp_w_down, gates, l))
        x = _combine(yp, idx, s0, cnt, x1, mod3, l, final_g, l == depth - 1, m_ctx, lat_len, cb, n_blocks * cb)

    y_prompt = x[0].reshape(n_ctx, ctx_len, d)
    y_sample = x[1].reshape(n_lat, lat_len, d)
    new_state_rglru = jnp.stack(rg_states, axis=1)
    new_state_ret = jnp.stack(ret_states, axis=1)
    return (y_prompt, y_sample, new_state_rglru, new_state_ret)
```

```python
import functools

import jax
import jax.numpy as jnp
from jax import lax
from jax.experimental import pallas as pl
from jax.experimental.pallas import tpu as pltpu

F32 = jnp.float32
BF16 = jnp.bfloat16

D_MODEL = 2048
D_CONV = 512
CONV_WIDTH = 31
D_RG = 512
RG_BLOCKS = 8
RG_CONV_WIDTH = 4
RG_C = 8.0
N_RET_HEADS = 8
RET_HEAD_DIM = 128
D_RET = N_RET_HEADS * RET_HEAD_DIM
RET_CHUNK = 128
ROPE_BASE = 10000.0
D_IN = 2 * D_CONV + 2 * D_RG + 4 * D_RET
N_EXPERTS = 16
EC_FACTOR = 2
N_MOD = 6
EPS = 1e-6
GRID_W = 64

V7X_VMEM_LIMIT_BYTES = 56 * 1024 * 1024
SEQ_TILE = 256
MOD_ROWS = 16


def _params(sem, vmem=V7X_VMEM_LIMIT_BYTES):
    return pltpu.CompilerParams(dimension_semantics=sem, vmem_limit_bytes=vmem)


def _silu(x):
    return x * jax.nn.sigmoid(x)


def _mod_kernel(c_ref, w_ref, b_ref, o_ref):
    c = _silu(c_ref[...]).astype(BF16)
    o_ref[0] = jnp.dot(c, w_ref[0].astype(BF16), preferred_element_type=F32) + b_ref[0]


def _modulation(cond, w_mod, b_mod):
    depth, d, n = w_mod.shape
    tn = 1024
    return pl.pallas_call(
        _mod_kernel,
        out_shape=jax.ShapeDtypeStruct((depth, MOD_ROWS, n), F32),
        grid=(depth, n // tn),
        in_specs=[
            pl.BlockSpec((MOD_ROWS, d), lambda l, j: (0, 0)),
            pl.BlockSpec((1, d, tn), lambda l, j: (l, 0, j)),
            pl.BlockSpec((1, 1, tn), lambda l, j: (l, 0, j)),
        ],
        out_specs=pl.BlockSpec((1, MOD_ROWS, tn), lambda l, j: (l, 0, j)),
        compiler_params=_params(("parallel", "parallel")),
        name="modulation",
    )(cond, w_mod, b_mod.reshape(depth, 1, n))


def _mod_row(m_ctx, seq_rows, tm):
    def f(i):
        start = i * tm
        return jnp.where(start < m_ctx, 0, 1 + (start - m_ctx) // seq_rows)
    return f


def _split_specs(x, m_ctx, tm, width, n_grid_axes):
    n_c = m_ctx // tm
    if not isinstance(x, tuple):
        if n_grid_axes == 1:
            return [x], [pl.BlockSpec((tm, width), lambda i: (i, 0))]
        return [x], [pl.BlockSpec((tm, width), lambda i, j: (i, 0))]
    xc, xl = x
    lat_blk = lambda i: jnp.maximum(i - n_c, 0)
    if n_grid_axes == 1:
        specs = [pl.BlockSpec((tm, width), lambda i: (jnp.minimum(i, n_c - 1), 0)),
                 pl.BlockSpec((tm, width), lambda i: (lat_blk(i), 0))]
    else:
        specs = [pl.BlockSpec((tm, width), lambda i, j: (jnp.minimum(i, n_c - 1), 0)),
                 pl.BlockSpec((tm, width), lambda i, j: (lat_blk(i), 0))]
    return [xc, xl], specs


def _pick(is_ctx, refs, rows=slice(None)):
    if len(refs) == 1:
        return refs[0][rows, :]
    return jnp.where(is_ctx, refs[0][rows, :], refs[1][rows, :])


NORM_ROWS = 16
NORM_UNROLL = 8


def _in_proj_kernel(n_ctx_blk, n_x, *refs):
    x_refs = refs[:n_x]
    g_ref, sc_ref, sh_ref, w_ref, o_ref, h_ref = refs[n_x:]

    @pl.when(pl.program_id(1) == 0)
    def _():
        is_ctx = pl.program_id(0) < n_ctx_blk
        g = g_ref[...]
        sc1 = 1.0 + sc_ref[0]
        sh = sh_ref[0]

        def body(r, carry):
            for k in range(NORM_UNROLL):
                rows = pl.ds(pl.multiple_of((r * NORM_UNROLL + k) * NORM_ROWS, NORM_ROWS), NORM_ROWS)
                x = _pick(is_ctx, x_refs, rows)
                y = x * lax.rsqrt(jnp.mean(x * x, axis=-1, keepdims=True) + EPS) * g
                h_ref[rows, :] = (y * sc1 + sh).astype(BF16)
            return carry

        lax.fori_loop(0, h_ref.shape[0] // (NORM_ROWS * NORM_UNROLL), body, 0)

    o_ref[...] = jnp.dot(h_ref[...], w_ref[...], preferred_element_type=F32)


def _in_proj(x, norm_g, mod3, layer, w_in_bf, m_ctx, lat_len, m):
    d, n = w_in_bf.shape
    tm, tn = (512, 2048) if isinstance(x, tuple) else (1024, 1024)
    row = _mod_row(m_ctx, lat_len, tm)
    base = layer * MOD_ROWS * N_MOD
    xs, xspecs = _split_specs(x, m_ctx, tm, d, 2)
    return pl.pallas_call(
        functools.partial(_in_proj_kernel, m_ctx // tm, len(xs)),
        out_shape=jax.ShapeDtypeStruct((m, n), F32),
        grid=(m // tm, n // tn),
        in_specs=xspecs + [
            pl.BlockSpec((1, d), lambda i, j: (0, 0)),
            pl.BlockSpec((1, 1, d), lambda i, j: (base + row(i) * N_MOD + 1, 0, 0)),
            pl.BlockSpec((1, 1, d), lambda i, j: (base + row(i) * N_MOD + 0, 0, 0)),
            pl.BlockSpec((d, tn), lambda i, j: (0, j)),
        ],
        out_specs=pl.BlockSpec((tm, tn), lambda i, j: (i, j)),
        scratch_shapes=[pltpu.VMEM((tm, d), BF16)],
        compiler_params=_params(("parallel", "arbitrary")),
        name="in_proj",
    )(*xs, norm_g.reshape(1, d), mod3, mod3, w_in_bf)


def _tile_flags(i, n_ctx_tiles, tiles_per_lat):
    j = jnp.maximum(i - n_ctx_tiles, 0) % tiles_per_lat
    is_ctx = i < n_ctx_tiles
    first = jnp.logical_or(is_ctx, j == 0)
    last = jnp.logical_or(is_ctx, j == tiles_per_lat - 1)
    return first, last


CONV_HALO = 16


def _conv_kernel(n_ctx_tiles, tiles_per_lat,
                 ca_ref, cb_ref, pa_ref, pb_ref, na_ref, nb_ref, w_ref, b_ref, lg_ref, lb_ref,
                 o_ref, u_ref, sh_ref):
    i = pl.program_id(0)
    first, last = _tile_flags(i, n_ctx_tiles, tiles_per_lat)
    t = SEQ_TILE
    hl = CONV_HALO
    u_ref[pl.ds(hl, t), :] = ca_ref[...] * jax.nn.sigmoid(cb_ref[...])
    prev = pa_ref[...] * jax.nn.sigmoid(pb_ref[...])
    nxt = na_ref[...] * jax.nn.sigmoid(nb_ref[...])
    u_ref[pl.ds(0, hl), :] = jnp.where(first, 0.0, prev)
    u_ref[pl.ds(hl + t, hl), :] = jnp.where(last, 0.0, nxt)

    sub = 8
    rows = t + 2 * hl - sub
    for s in range(1, sub):
        sh_ref[s - 1, pl.ds(0, rows), :] = u_ref[pl.ds(s, rows), :]

    pad = CONV_WIDTH // 2
    lane = 128
    cols = []
    for c in range(D_CONV // lane):
        acc = jnp.zeros((t, lane), F32)
        for k in range(CONV_WIDTH):
            off = hl - pad + k
            s, base = off % sub, off - off % sub
            src = u_ref if s == 0 else sh_ref.at[s - 1]
            acc = acc + src[pl.ds(base, t), pl.ds(c * lane, lane)] * w_ref[pl.ds(k, 1), pl.ds(c * lane, lane)]
        cols.append(acc)
    y = jnp.concatenate(cols, axis=-1) + b_ref[...]
    mu = jnp.mean(y, axis=-1, keepdims=True)
    var = jnp.mean(jnp.square(y - mu), axis=-1, keepdims=True)
    z = (y - mu) * lax.rsqrt(var + EPS) * lg_ref[...] + lb_ref[...]
    o_ref[...] = _silu(z).astype(BF16)


def _conv_mixer(proj, conv_w, conv_b, ln_g, ln_b, n_ctx_tiles, tiles_per_lat):
    m = proj.shape[0]
    t, hl = SEQ_TILE, CONV_HALO
    r = t // hl
    n_tiles = m // t
    last_halo = m // hl - 1
    w_pad = jnp.zeros((32, D_CONV), F32).at[:CONV_WIDTH].set(conv_w)
    row = lambda v: v.reshape(1, D_CONV)
    cur = lambda c: pl.BlockSpec((t, D_CONV), lambda i: (i, c))
    prv = lambda c: pl.BlockSpec((hl, D_CONV), lambda i: (jnp.maximum(i * r - 1, 0), c))
    nxt = lambda c: pl.BlockSpec((hl, D_CONV), lambda i: (jnp.minimum((i + 1) * r, last_halo), c))
    vec = pl.BlockSpec((1, D_CONV), lambda i: (0, 0))
    return pl.pallas_call(
        functools.partial(_conv_kernel, n_ctx_tiles, tiles_per_lat),
        out_shape=jax.ShapeDtypeStruct((m, D_CONV), BF16),
        grid=(n_tiles,),
        in_specs=[cur(0), cur(1), prv(0), prv(1), nxt(0), nxt(1),
                  pl.BlockSpec((32, D_CONV), lambda i: (0, 0)), vec, vec, vec],
        out_specs=pl.BlockSpec((t, D_CONV), lambda i: (i, 0)),
        scratch_shapes=[pltpu.VMEM((t + 2 * hl, D_CONV), F32), pltpu.VMEM((7, t + 2 * hl, D_CONV), F32)],
        compiler_params=_params(("parallel",)),
        name="conv_mixer",
    )(proj, proj, proj, proj, proj, proj, w_pad, row(conv_b), row(ln_g), row(ln_b))


RG_HALO = 8


def _shift_rows(x, s, fill, reverse, row):
    t = x.shape[0]
    if s % 8 == 0:
        pad = jnp.full((s, x.shape[1]), fill, x.dtype)
        return jnp.concatenate([x[s:], pad] if reverse else [pad, x[:t - s]], axis=0)
    if reverse:
        return jnp.where(row < t - s, pltpu.roll(x, t - s, axis=0), fill)
    return jnp.where(row >= s, pltpu.roll(x, s, axis=0), fill)


def _rglru_kernel(reverse, with_out, n_ctx_tiles, tiles_per_lat, *refs):
    if with_out:
        (x_ref, xp_ref, xn_ref, cw_ref, cb_ref, wg_ref, bg_ref, lam_ref, h0_ref, gate_ref, hf_ref,
         o_ref, st_ref, carry_ref, ext_ref) = refs
    else:
        (x_ref, xp_ref, xn_ref, cw_ref, cb_ref, wg_ref, bg_ref, lam_ref, h0_ref,
         o_ref, st_ref, carry_ref, ext_ref) = refs
    n_tiles = pl.num_programs(0)
    step = pl.program_id(0)
    i = (n_tiles - 1 - step) if reverse else step
    first, last = _tile_flags(i, n_ctx_tiles, tiles_per_lat)
    t = SEQ_TILE

    ext_ref[pl.ds(0, RG_HALO), :] = jnp.where(first, 0.0, xp_ref[...])
    ext_ref[pl.ds(RG_HALO, t), :] = x_ref[...]
    ext_ref[pl.ds(RG_HALO + t, RG_HALO), :] = jnp.where(last, 0.0, xn_ref[...])
    xr = cb_ref[...] + jnp.zeros((t, D_RG), F32)
    for k in range(RG_CONV_WIDTH):
        off = RG_HALO - RG_CONV_WIDTH // 2 + k
        xr = xr + ext_ref[pl.ds(off, t), :] * cw_ref[pl.ds(k, 1), :]

    pre = jnp.dot(xr.astype(BF16), wg_ref[...], preferred_element_type=F32) + bg_ref[...]
    r = jax.nn.sigmoid(pre[:, :D_RG])
    ig = jax.nn.sigmoid(pre[:, D_RG:])
    nl = -lam_ref[...]
    softplus = jnp.maximum(nl, 0.0) + jnp.log1p(jnp.exp(-jnp.abs(nl)))
    log_a = -RG_C * r * softplus
    a = jnp.exp(log_a)
    u = jnp.sqrt(-jnp.tanh(log_a) * (a * a + 1.0)) * (ig * xr)

    row = lax.broadcasted_iota(jnp.int32, (t, D_RG), 0)
    s = 1
    while s < t:
        a_s = _shift_rows(a, s, 1.0, reverse, row)
        u_s = _shift_rows(u, s, 0.0, reverse, row)
        u = a * u_s + u
        a = a * a_s
        s *= 2

    seq_start = last if reverse else first
    h_in = jnp.where(seq_start, h0_ref[0], carry_ref[...])
    h = a * h_in + u
    end = h[0:1] if reverse else h[t - 1:t]
    carry_ref[...] = end
    st_ref[0] = end
    if with_out:
        o_ref[...] = (jax.nn.gelu(gate_ref[...]) * (hf_ref[...] + h)).astype(BF16)
    else:
        o_ref[...] = h


def _seq_of_tile(n_ctx_tiles, tiles_per_lat):
    def f(i):
        return jnp.where(i < n_ctx_tiles, i, n_ctx_tiles + (i - n_ctx_tiles) // tiles_per_lat)
    return f


def _rglru_dir(proj, hf, conv_w, conv_b, w_gate_bf, b_gate, lam, h0, reverse, n_ctx_tiles, tiles_per_lat):
    m = proj.shape[0]
    t, hl = SEQ_TILE, RG_HALO
    r = t // hl
    n_tiles = m // t
    last_halo = m // hl - 1
    with_out = hf is not None
    tile = (lambda s: n_tiles - 1 - s) if reverse else (lambda s: s)
    seq = _seq_of_tile(n_ctx_tiles, tiles_per_lat)
    col_x = (2 * D_CONV + D_RG) // D_RG
    col_g = (2 * D_CONV) // D_RG
    cw_pad = jnp.zeros((8, D_RG), F32).at[:RG_CONV_WIDTH].set(conv_w)
    in_specs = [
        pl.BlockSpec((t, D_RG), lambda s: (tile(s), col_x)),
        pl.BlockSpec((hl, D_RG), lambda s: (jnp.maximum(tile(s) * r - 1, 0), col_x)),
        pl.BlockSpec((hl, D_RG), lambda s: (jnp.minimum((tile(s) + 1) * r, last_halo), col_x)),
        pl.BlockSpec((8, D_RG), lambda s: (0, 0)),
        pl.BlockSpec((1, D_RG), lambda s: (0, 0)),
        pl.BlockSpec((D_RG, 2 * D_RG), lambda s: (0, 0)),
        pl.BlockSpec((1, 2 * D_RG), lambda s: (0, 0)),
        pl.BlockSpec((1, D_RG), lambda s: (0, 0)),
        pl.BlockSpec((1, 1, D_RG), lambda s: (seq(tile(s)), 0, 0)),
    ]
    args = [proj, proj, proj, cw_pad, conv_b.reshape(1, D_RG), w_gate_bf, b_gate.reshape(1, 2 * D_RG),
            lam.reshape(1, D_RG), h0]
    if with_out:
        in_specs += [pl.BlockSpec((t, D_RG), lambda s: (tile(s), col_g)),
                     pl.BlockSpec((t, D_RG), lambda s: (tile(s), 0))]
        args += [proj, hf]
    out_dtype = BF16 if with_out else F32
    return pl.pallas_call(
        functools.partial(_rglru_kernel, reverse, with_out, n_ctx_tiles, tiles_per_lat),
        out_shape=(jax.ShapeDtypeStruct((m, D_RG), out_dtype),
                   jax.ShapeDtypeStruct((n_tiles, 1, D_RG), F32)),
        grid=(n_tiles,),
        in_specs=in_specs,
        out_specs=(pl.BlockSpec((t, D_RG), lambda s: (tile(s), 0)),
                   pl.BlockSpec((1, 1, D_RG), lambda s: (tile(s), 0, 0))),
        scratch_shapes=[pltpu.VMEM((1, D_RG), F32), pltpu.VMEM((t + 2 * hl, D_RG), F32)],
        compiler_params=_params(("arbitrary",)),
        name="rglru_bwd" if reverse else "rglru_fwd",
    )(*args)


def _block_diag(w):
    g, bi, bj = w.shape
    eye = jnp.eye(g, dtype=w.dtype)
    return (eye[:, None, :, None] * w[:, :, None, :]).reshape(g * bi, g * bj)


RET_GROUP = 4
RET_OUT_GROUP = 8


def _ret_kernel(n_chunks, rotary, q_ref, k_ref, v_ref, g_ref, lg_ref, gn_ref, s0_ref, *rest):
    if rotary:
        cos_ref, sin_ref, o_ref, st_ref, qb_ref, kb_ref, sf_ref, sb_ref, kvf_ref, kvb_ref = rest
    else:
        o_ref, st_ref, qb_ref, kb_ref, sf_ref, sb_ref, kvf_ref, kvb_ref = rest
    c = RET_CHUNK
    dh = RET_HEAD_DIM
    scale = RET_HEAD_DIM ** -0.5

    q = q_ref[...]
    k = k_ref[...] * scale
    if rotary:
        lane = lax.broadcasted_iota(jnp.int32, q.shape, 1)
        low = (lane % (dh // 2)) < (dh // 4)
        cos = cos_ref[...]
        sin = sin_ref[...]

        def rot(x):
            partner = jnp.where(low, pltpu.roll(x, dh - dh // 4, axis=1), pltpu.roll(x, dh // 4, axis=1))
            return x * cos + partner * sin
        q = rot(q)
        k = rot(k)
    qb_ref[...] = q
    kb_ref[...] = k

    lgf = lg_ref[0]
    lgb = lg_ref[1]
    ii = lax.broadcasted_iota(jnp.int32, (c, c), 0).astype(F32)
    jj = lax.broadcasted_iota(jnp.int32, (c, c), 1).astype(F32)
    rel = ii - jj
    decay = (jnp.where(rel >= 0, jnp.exp(jnp.maximum(rel, 0.0) * lgf), 0.0)
             + jnp.where(rel <= 0, jnp.exp(jnp.maximum(-rel, 0.0) * lgb), 0.0))
    qwf = jnp.exp((ii + 1.0) * lgf)
    kwf = jnp.exp((c - 1.0 - ii) * lgf)
    qwb = jnp.exp((c - ii) * lgb)
    kwb = jnp.exp(ii * lgb)
    cdf = jnp.exp(c * lgf)
    cdb = jnp.exp(c * lgb)

    def chunk(ci):
        return pl.ds(pl.multiple_of(ci * c, c), c)

    group = min(RET_GROUP, n_chunks)

    def kv_group(gi, carry):
        cis = [gi * group + j for j in range(group)]
        dims = (((0,), (0,)), ((), ()))
        ops = []
        for ci in cis:
            kc = kb_ref[chunk(ci), :]
            ops.append(((kc * kwf).astype(BF16), (kc * kwb).astype(BF16), v_ref[chunk(ci), :].astype(BF16)))
        for ci, (kf, kb, vc) in zip(cis, ops):
            kvf_ref[ci] = lax.dot_general(kf, vc, dims, preferred_element_type=F32)
            kvb_ref[ci] = lax.dot_general(kb, vc, dims, preferred_element_type=F32)
        return carry

    lax.fori_loop(0, n_chunks // group, kv_group, 0)

    def fwd(ci, s):
        sf_ref[ci] = s.astype(BF16)
        return cdf * s + kvf_ref[ci]

    def bwd(step, s):
        ci = n_chunks - 1 - step
        sb_ref[ci] = s.astype(BF16)
        return cdb * s + kvb_ref[ci]

    st_ref[0, 0, 0] = lax.fori_loop(0, n_chunks, fwd, s0_ref[0, 0, 0])
    st_ref[0, 1, 0] = lax.fori_loop(0, n_chunks, bwd, s0_ref[0, 1, 0])

    gn = gn_ref[...]

    out_grp = min(RET_OUT_GROUP, n_chunks)

    def out_group(gi, carry):
        cis = [gi * out_grp + j for j in range(out_grp)]
        qfs = [qb_ref[chunk(ci), :] for ci in cis]
        scores = [lax.dot_general(qf.astype(BF16), kb_ref[chunk(ci), :].astype(BF16), (((1,), (1,)), ((), ())),
                                  preferred_element_type=F32) for ci, qf in zip(cis, qfs)]
        outs = []
        for ci, qf, sc in zip(cis, qfs, scores):
            lhs = jnp.concatenate([(sc * decay).astype(BF16), (qf * qwf).astype(BF16), (qf * qwb).astype(BF16)],
                                  axis=1)
            rhs = jnp.concatenate([v_ref[chunk(ci), :].astype(BF16), sf_ref[ci], sb_ref[ci]], axis=0)
            outs.append(jnp.dot(lhs, rhs, preferred_element_type=F32))
        for ci, o in zip(cis, outs):
            mu = jnp.mean(o, axis=-1, keepdims=True)
            var = jnp.mean(jnp.square(o - mu), axis=-1, keepdims=True)
            o = (o - mu) * lax.rsqrt(var + EPS) * gn
            o_ref[chunk(ci), :] = (_silu(g_ref[chunk(ci), :]) * o).astype(BF16)
        return carry

    lax.fori_loop(0, n_chunks // out_grp, out_group, 0)


def _retention(proj, log_g3, gn_g, s0, tables, row_blk0, n_seq, length, layer):
    dh, nh = RET_HEAD_DIM, N_RET_HEADS
    rotary = tables is not None
    col0 = (2 * D_CONV + 2 * D_RG) // dh
    blk = lambda part: pl.BlockSpec((length, dh), lambda b, h: (row_blk0 + b, col0 + part * nh + h))
    in_specs = [blk(0), blk(1), blk(2), blk(3),
                pl.BlockSpec((2, 1, dh), lambda b, h: ((layer * nh + h), 0, 0)),
                pl.BlockSpec((1, dh), lambda b, h: (0, h)),
                pl.BlockSpec((1, 2, 1, dh, dh), lambda b, h: (b, 0, h, 0, 0))]
    args = [proj, proj, proj, proj, log_g3, gn_g.reshape(1, D_RET), s0]
    if rotary:
        tab = pl.BlockSpec((length, dh), lambda b, h: (0, 0))
        in_specs += [tab, tab]
        args += list(tables)
    n_chunks = length // RET_CHUNK
    return pl.pallas_call(
        functools.partial(_ret_kernel, n_chunks, rotary),
        out_shape=(jax.ShapeDtypeStruct((n_seq * length, D_RET), BF16),
                   jax.ShapeDtypeStruct((n_seq, 2, nh, dh, dh), F32)),
        grid=(n_seq, nh),
        in_specs=in_specs,
        out_specs=(pl.BlockSpec((length, dh), lambda b, h: (b, h)),
                   pl.BlockSpec((1, 2, 1, dh, dh), lambda b, h: (b, 0, h, 0, 0))),
        scratch_shapes=[pltpu.VMEM((length, dh), F32), pltpu.VMEM((length, dh), F32),
                        pltpu.VMEM((n_chunks, dh, dh), BF16), pltpu.VMEM((n_chunks, dh, dh), BF16),
                        pltpu.VMEM((n_chunks, dh, dh), F32), pltpu.VMEM((n_chunks, dh, dh), F32)],
        compiler_params=_params(("parallel", "parallel")),
        name="retention_lat" if rotary else "retention_ctx",
    )(*args)


def _rope_tables(length):
    dh = RET_HEAD_DIM
    quarter = dh // 4
    pos = jnp.arange(length)
    rows = (pos // GRID_W).astype(F32)
    cols = (pos % GRID_W).astype(F32)
    freq = ROPE_BASE ** (-jnp.arange(quarter, dtype=F32) / quarter)
    ang_r = rows[:, None] * freq[None, :]
    ang_c = cols[:, None] * freq[None, :]
    cos = jnp.concatenate([jnp.cos(ang_r), jnp.cos(ang_r), jnp.cos(ang_c), jnp.cos(ang_c)], axis=-1)
    sin = jnp.concatenate([-jnp.sin(ang_r), jnp.sin(ang_r), -jnp.sin(ang_c), jnp.sin(ang_c)], axis=-1)
    return cos, sin


LANES = 128
PACK_ROWS = D_MODEL // (2 * LANES)
CHUNK_ROWS = D_MODEL // LANES
HIGH_HALF = 0xFFFF0000


def _pack_bf16_pairs(x):
    half = x.shape[1] // 2
    lo = lax.bitcast_convert_type(x[:, :half].astype(BF16).astype(F32), jnp.uint32) >> 16
    hi = lax.bitcast_convert_type(x[:, half:].astype(BF16).astype(F32), jnp.uint32) & jnp.uint32(HIGH_HALF)
    return lo | hi


def _unpack_low(p):
    return lax.bitcast_convert_type(p << 16, F32)


def _unpack_high(p):
    return lax.bitcast_convert_type(p & jnp.uint32(HIGH_HALF), F32)


def _store_token_rows(ref, v):
    tm, n = v.shape[0], v.shape[1] // LANES
    for j in range(n):
        ref[pl.ds(j, tm, stride=n), :] = v[:, j * LANES:(j + 1) * LANES]


def _load_token_rows(ref, tm, n):
    return jnp.concatenate([ref[pl.ds(j, tm, stride=n), :] for j in range(n)], axis=-1)


def _out_proj_kernel(n_ctx_blk, n_x, u_ref, r_ref, oc_ref, ol_ref, w_ref, *refs):
    x_refs = refs[:n_x]
    g1_ref, ng_ref, sc_ref, sh_ref, rw_ref, x1_ref, h2_ref, aff_ref = refs[n_x:]
    is_ctx = pl.program_id(0) < n_ctx_blk
    lhs = jnp.concatenate([u_ref[...], r_ref[...], _pick(is_ctx, (oc_ref, ol_ref))], axis=-1)
    y = jnp.dot(lhs, w_ref[...], preferred_element_type=F32)
    x1 = _pick(is_ctx, x_refs) + g1_ref[0] * y
    x1_ref[...] = x1
    n = x1 * lax.rsqrt(jnp.mean(x1 * x1, axis=-1, keepdims=True) + EPS) * ng_ref[...]
    h2f = n * (1.0 + sc_ref[0]) + sh_ref[0]
    _store_token_rows(h2_ref, _pack_bf16_pairs(h2f))
    h2 = h2f.astype(BF16)
    logits = lax.dot_general(rw_ref[...], h2, (((1,), (1,)), ((), ())), preferred_element_type=F32)
    z = logits - jnp.max(logits, axis=0, keepdims=True)
    e = jnp.exp(z)
    aff_ref[...] = e / jnp.sum(e, axis=0, keepdims=True)


def _out_proj(u, r, o, w_out_bf, x, norm_g, mod3, layer, router_wt_bf, m_ctx, lat_len):
    m, d = u.shape[0], w_out_bf.shape[1]
    tm = 512
    row = _mod_row(m_ctx, lat_len, tm)
    base = layer * MOD_ROWS * N_MOD
    modspec = lambda j: pl.BlockSpec((1, 1, d), lambda i: (base + row(i) * N_MOD + j, 0, 0))
    os_, ospecs = _split_specs(o, m_ctx, tm, D_RET, 1)
    xs, xspecs = _split_specs(x, m_ctx, tm, d, 1)
    return pl.pallas_call(
        functools.partial(_out_proj_kernel, m_ctx // tm, len(xs)),
        out_shape=(jax.ShapeDtypeStruct((m, d), F32),
                   jax.ShapeDtypeStruct((m * PACK_ROWS, LANES), jnp.uint32),
                   jax.ShapeDtypeStruct((N_EXPERTS, m), F32)),
        grid=(m // tm,),
        in_specs=[
            pl.BlockSpec((tm, D_CONV), lambda i: (i, 0)),
            pl.BlockSpec((tm, D_RG), lambda i: (i, 0)),
            *ospecs,
            pl.BlockSpec((d, d), lambda i: (0, 0)),
            *xspecs,
            modspec(2),
            pl.BlockSpec((1, d), lambda i: (0, 0)),
            modspec(4), modspec(3),
            pl.BlockSpec((N_EXPERTS, d), lambda i: (0, 0)),
        ],
        out_specs=(pl.BlockSpec((tm, d), lambda i: (i, 0)),
                   pl.BlockSpec((tm * PACK_ROWS, LANES), lambda i: (i, 0)),
                   pl.BlockSpec((N_EXPERTS, tm), lambda i: (0, i))),
        compiler_params=_params(("parallel",)),
        name="out_proj",
    )(u, r, *os_, w_out_bf, *xs, mod3, norm_g.reshape(1, d), mod3, mod3, router_wt_bf)


def _ffn_up_kernel(x_ref, wg_ref, wu_ref, h_ref, wgb_ref, wub_ref):
    @pl.when(pl.program_id(2) == 0)
    def _():
        wgb_ref[...] = wg_ref[0, 0].astype(BF16)
        wub_ref[...] = wu_ref[0, 0].astype(BF16)

    tm = h_ref.shape[1]
    p = _load_token_rows(x_ref.at[0], tm, PACK_ROWS)
    x = jnp.concatenate([_unpack_low(p), _unpack_high(p)], axis=-1).astype(BF16)
    g = jnp.dot(x, wgb_ref[...], preferred_element_type=F32)
    u = jnp.dot(x, wub_ref[...], preferred_element_type=F32)
    h_ref[0] = (_silu(g) * u).astype(BF16)


def _ffn_up(xg_packed, w_gate, w_up, layer):
    e = xg_packed.shape[0]
    me = xg_packed.shape[1] // PACK_ROWS
    d, f = w_gate.shape[-2:]
    tm, tn = me // 4, 512
    wspec = pl.BlockSpec((1, 1, d, tn), lambda ei, j, i: (layer, ei, 0, j))
    return pl.pallas_call(
        _ffn_up_kernel,
        out_shape=jax.ShapeDtypeStruct((e, me, f), BF16),
        grid=(e, f // tn, me // tm),
        in_specs=[pl.BlockSpec((1, tm * PACK_ROWS, LANES), lambda ei, j, i: (ei, i, 0)), wspec, wspec],
        out_specs=pl.BlockSpec((1, tm, tn), lambda ei, j, i: (ei, i, j)),
        scratch_shapes=[pltpu.VMEM((d, tn), BF16), pltpu.VMEM((d, tn), BF16)],
        compiler_params=_params(("parallel", "parallel", "arbitrary")),
        name="ffn_up",
    )(xg_packed, w_gate, w_up)


def _ffn_down_kernel(h_ref, w_ref, g_ref, y_ref, wb_ref):
    @pl.when(pl.program_id(2) == 0)
    def _():
        wb_ref[...] = w_ref[0, 0].astype(BF16)

    y = jnp.dot(h_ref[0], wb_ref[...], preferred_element_type=F32)
    y_ref[0] = (y * g_ref[0]).astype(BF16)


def _ffn_down(hid, w_down, gates, layer):
    e, me, f = hid.shape
    d = w_down.shape[-1]
    tm, tn = me // 6, 512
    return pl.pallas_call(
        _ffn_down_kernel,
        out_shape=jax.ShapeDtypeStruct((e, me, d), BF16),
        grid=(e, d // tn, me // tm),
        in_specs=[pl.BlockSpec((1, tm, f), lambda ei, j, i: (ei, i, 0)),
                  pl.BlockSpec((1, 1, f, tn), lambda ei, j, i: (layer, ei, 0, j)),
                  pl.BlockSpec((1, tm, 1), lambda ei, j, i: (ei, i, 0))],
        out_specs=pl.BlockSpec((1, tm, tn), lambda ei, j, i: (ei, i, j)),
        scratch_shapes=[pltpu.VMEM((f, tn), BF16)],
        compiler_params=_params(("parallel", "parallel", "arbitrary")),
        name="ffn_down",
    )(hid, w_down, gates.reshape(e, me, 1))


def _pack_y_kernel(y_ref, o_ref):
    _store_token_rows(o_ref, _pack_bf16_pairs(y_ref[0]))


def _pack_y(y):
    e, me, d = y.shape
    tm = me // 8
    return pl.pallas_call(
        _pack_y_kernel,
        out_shape=jax.ShapeDtypeStruct((e * me * PACK_ROWS, LANES), jnp.uint32),
        grid=(e, me // tm),
        in_specs=[pl.BlockSpec((1, tm, d), lambda ei, i: (ei, i, 0))],
        out_specs=pl.BlockSpec((tm * PACK_ROWS, LANES), lambda ei, i: (ei * (me // tm) + i, 0)),
        compiler_params=_params(("parallel", "parallel")),
        name="pack_y",
    )(y)


CUMSUM_CHUNK = 256
SELECT_GROUP = 8


def _exclusive_cumsum(x, tri):
    r, sl = x.shape
    carry = jnp.zeros((r, 1), F32)
    out = []
    for k in range(sl // CUMSUM_CHUNK):
        xk = x[:, k * CUMSUM_CHUNK:(k + 1) * CUMSUM_CHUNK]
        inc = jnp.dot(xk.astype(BF16), tri, preferred_element_type=F32) + carry
        out.append(inc - xk)
        carry = inc[:, CUMSUM_CHUNK - 1:CUMSUM_CHUNK]
    return jnp.concatenate(out, axis=-1)


def _topk_kernel(cap, a_ref, idx_ref, gate_ref, pos_ref):
    a = a_ref[...]
    r, sl = a.shape
    bits = lax.bitcast_convert_type(a, jnp.int32)

    thr = jnp.zeros((r, 1), jnp.int32)
    for bit in range(30, -1, -1):
        cand = thr | (1 << bit)
        cnt = jnp.sum(jnp.where(bits >= cand, 1, 0), axis=1, keepdims=True)
        thr = jnp.where(cnt >= cap, cand, thr)

    ci = lax.broadcasted_iota(jnp.int32, (CUMSUM_CHUNK, CUMSUM_CHUNK), 0)
    cj = lax.broadcasted_iota(jnp.int32, (CUMSUM_CHUNK, CUMSUM_CHUNK), 1)
    tri = jnp.where(ci <= cj, 1.0, 0.0).astype(BF16)

    gt = bits > thr
    eq = jnp.where(bits == thr, 1.0, 0.0)
    need = (cap - jnp.sum(jnp.where(gt, 1, 0), axis=1, keepdims=True)).astype(F32)
    eq_rank = _exclusive_cumsum(eq, tri)
    sel = jnp.where(gt, 1.0, jnp.where(eq_rank < need, eq, 0.0))
    pos = _exclusive_cumsum(sel, tri)
    pos_ref[...] = pos.astype(jnp.int32)

    pos_sel = jnp.where(sel > 0.0, pos, -1.0)
    tok = lax.broadcasted_iota(jnp.int32, (r, sl), 1).astype(F32)
    capp = idx_ref.shape[1]
    slot = lax.broadcasted_iota(jnp.int32, (r, capp), 1)
    idx_ref[...] = jnp.zeros((r, capp), jnp.int32)
    gate_ref[...] = jnp.zeros((r, capp), F32)

    def compact(g, carry):
        s0 = g * SELECT_GROUP
        found = []
        for k in range(SELECT_GROUP):
            m = pos_sel == jnp.asarray(s0 + k, F32)
            found.append((jnp.sum(jnp.where(m, tok, 0.0), axis=1, keepdims=True),
                          jnp.sum(jnp.where(m, a, 0.0), axis=1, keepdims=True)))
        idx_new = idx_ref[...]
        gate_new = gate_ref[...]
        for k, (t_s, g_s) in enumerate(found):
            hit = slot == s0 + k
            idx_new = jnp.where(hit, t_s.astype(jnp.int32), idx_new)
            gate_new = jnp.where(hit, g_s, gate_new)
        idx_ref[...] = idx_new
        gate_ref[...] = gate_new
        return carry

    lax.fori_loop(0, cap // SELECT_GROUP, compact, 0)


def _select(aff, cap, r_blk, n_steps, sl, col_blk_of):
    capp = max(cap, LANES)
    return pl.pallas_call(
        functools.partial(_topk_kernel, cap),
        out_shape=(jax.ShapeDtypeStruct((n_steps * r_blk, capp), jnp.int32),
                   jax.ShapeDtypeStruct((n_steps * r_blk, capp), F32),
                   jax.ShapeDtypeStruct((n_steps * r_blk, sl), jnp.int32)),
        grid=(n_steps,),
        in_specs=[pl.BlockSpec((r_blk, sl), col_blk_of)],
        out_specs=(pl.BlockSpec((r_blk, capp), lambda i: (i, 0)),
                   pl.BlockSpec((r_blk, capp), lambda i: (i, 0)),
                   pl.BlockSpec((r_blk, sl), lambda i: (i, 0))),
        compiler_params=_params(("parallel",)),
        name="expert_select",
    )(aff)


GATHER_UNROLL = 8


def _dispatch_kernel(cb, idx_ref, x_ref, o_ref):
    def body(g, carry):
        for k in range(GATHER_UNROLL):
            s = g * GATHER_UNROLL + k
            t = idx_ref[0, 0, s]
            o_ref[pl.ds(pl.multiple_of(s * PACK_ROWS, PACK_ROWS), PACK_ROWS), :] = (
                x_ref[pl.ds(pl.multiple_of(t * PACK_ROWS, PACK_ROWS), PACK_ROWS), :])
        return carry

    lax.fori_loop(0, cb // GATHER_UNROLL, body, 0)


def _dispatch(h2_packed, idx, n_blocks, blk_len, cb):
    e = N_EXPERTS
    out = pl.pallas_call(
        functools.partial(_dispatch_kernel, cb),
        out_shape=jax.ShapeDtypeStruct((e * n_blocks * cb * PACK_ROWS, LANES), jnp.uint32),
        grid=(n_blocks, e),
        in_specs=[pl.BlockSpec((1, 1, cb), lambda b, ei: (b * e + ei, 0, 0), memory_space=pltpu.SMEM),
                  pl.BlockSpec((blk_len * PACK_ROWS, LANES), lambda b, ei: (b, 0))],
        out_specs=pl.BlockSpec((cb * PACK_ROWS, LANES), lambda b, ei: (ei * n_blocks + b, 0)),
        compiler_params=_params(("parallel", "arbitrary")),
        name="dispatch",
    )(idx, h2_packed)
    return out.reshape(e, n_blocks * cb * PACK_ROWS, LANES)


COMBINE_WINDOW = 64


COMBINE_ROWS = 4


def _combine_kernel(final, cb, tiles_per_block, me, n_ctx_tiles, s0_ref, n_ref, idx_ref, yp_ref, x1_ref, g2_ref,
                    fg_ref, *rest):
    if final:
        oc_ref, ol_ref, ybuf, acc, sem = rest
    else:
        o_ref, ybuf, acc, sem = rest
    e_n = N_EXPERTS
    w = COMBINE_WINDOW
    t = SEQ_TILE
    i = pl.program_id(0)
    n_tiles = pl.num_programs(0)
    par = i % 2
    tile_base = (i % tiles_per_block) * t
    acc[...] = jnp.zeros(acc.shape, F32)

    def window(tile, slot, e, start):
        ws = jnp.minimum(start, cb - w)
        row0 = pl.multiple_of((e * me + (tile // tiles_per_block) * cb + ws) * PACK_ROWS, PACK_ROWS)
        return ws, pltpu.make_async_copy(yp_ref.at[pl.ds(row0, w * PACK_ROWS), :], ybuf.at[slot, e],
                                         sem.at[slot, e])

    def prefetch(tile, slot):
        for e in range(e_n):
            window(tile, slot, e, s0_ref[tile * e_n + e])[1].start()

    @pl.when(i == 0)
    def _():
        prefetch(i, par)

    @pl.when(i + 1 < n_tiles)
    def _():
        prefetch(i + 1, 1 - par)

    for e in range(e_n):
        s0 = s0_ref[i * e_n + e]
        end = s0 + n_ref[i * e_n + e]

        def chunk(start, prefetched, e=e, end=end):
            ws, cp = window(i, par, e, start)
            if not prefetched:
                cp.start()
            cp.wait()
            cnt = jnp.minimum(end, ws + w) - start

            def rows(q, carry):
                loaded = []
                for k in range(COMBINE_ROWS):
                    r = q * COMBINE_ROWS + k
                    slot = start + jnp.minimum(r, cnt - 1)
                    tok = jnp.where(r < cnt, idx_ref[e, 0, slot] - tile_base, t)
                    y = ybuf[par, e, pl.ds(pl.multiple_of((slot - ws) * PACK_ROWS, PACK_ROWS), PACK_ROWS), :]
                    base = pl.multiple_of(tok * CHUNK_ROWS, CHUNK_ROWS)
                    loaded.append((base, acc[pl.ds(base, PACK_ROWS), :] + _unpack_low(y),
                                   acc[pl.ds(base + PACK_ROWS, PACK_ROWS), :] + _unpack_high(y)))
                for base, lo, hi in loaded:
                    acc[pl.ds(base, PACK_ROWS), :] = lo
                    acc[pl.ds(base + PACK_ROWS, PACK_ROWS), :] = hi
                return carry

            lax.fori_loop(0, (cnt + COMBINE_ROWS - 1) // COMBINE_ROWS, rows, 0)
            return start + cnt

        nxt = chunk(s0, True)
        lax.while_loop(lambda st, end=end: st < end, lambda st: chunk(st, False), nxt)

    moe = _load_token_rows(acc, t, CHUNK_ROWS)
    x2 = x1_ref[...] + g2_ref[0] * moe
    if final:
        x2 = x2 * lax.rsqrt(jnp.mean(x2 * x2, axis=-1, keepdims=True) + EPS) * fg_ref[...]

        @pl.when(i < n_ctx_tiles)
        def _():
            oc_ref[...] = x2

        @pl.when(i >= n_ctx_tiles)
        def _():
            ol_ref[...] = x2
    else:
        o_ref[...] = x2


def _combine(y_packed, idx, s0, cnt, x1, mod3, layer, final_g, final, m_ctx, lat_len, cb, me):
    m, d = x1.shape
    t = SEQ_TILE
    tiles_per_block = lat_len // t
    n_ctx_tiles = m_ctx // t
    row = _mod_row(m_ctx, lat_len, t)
    base = layer * MOD_ROWS * N_MOD
    if final:
        out_shape = (jax.ShapeDtypeStruct((m_ctx, d), F32), jax.ShapeDtypeStruct((m - m_ctx, d), F32))
        out_specs = (pl.BlockSpec((t, d), lambda i, s0r, nr: (jnp.minimum(i, n_ctx_tiles - 1), 0)),
                     pl.BlockSpec((t, d), lambda i, s0r, nr: (jnp.maximum(i - n_ctx_tiles, 0), 0)))
    else:
        out_shape = jax.ShapeDtypeStruct((m, d), F32)
        out_specs = pl.BlockSpec((t, d), lambda i, s0r, nr: (i, 0))
    grid_spec = pltpu.PrefetchScalarGridSpec(
        num_scalar_prefetch=2,
        grid=(m // t,),
        in_specs=[
            pl.BlockSpec((N_EXPERTS, 1, cb), lambda i, s0r, nr: (i // tiles_per_block, 0, 0),
                         memory_space=pltpu.SMEM),
            pl.BlockSpec(memory_space=pl.ANY),
            pl.BlockSpec((t, d), lambda i, s0r, nr: (i, 0)),
            pl.BlockSpec((1, 1, d), lambda i, s0r, nr: (base + row(i) * N_MOD + 5, 0, 0)),
            pl.BlockSpec((1, d), lambda i, s0r, nr: (0, 0)),
        ],
        out_specs=out_specs,
        scratch_shapes=[pltpu.VMEM((2, N_EXPERTS, COMBINE_WINDOW * PACK_ROWS, LANES), jnp.uint32),
                        pltpu.VMEM(((t + 1) * CHUNK_ROWS, LANES), F32),
                        pltpu.SemaphoreType.DMA((2, N_EXPERTS))],
    )
    return pl.pallas_call(
        functools.partial(_combine_kernel, final, cb, tiles_per_block, me, n_ctx_tiles),
        out_shape=out_shape,
        grid_spec=grid_spec,
        compiler_params=_params(("arbitrary",)),
        name="combine",
    )(s0, cnt, idx, y_packed, x1, mod3, final_g.reshape(1, d))


def _route(aff_t, n_ctx, ctx_len, n_lat, lat_len):
    e = N_EXPERTS
    m_ctx = n_ctx * ctx_len
    cap_c = EC_FACTOR * ctx_len // e
    cb = EC_FACTOR * lat_len // e
    spb = lat_len // ctx_len
    n_cblk = n_ctx // spb
    tpb = lat_len // SEQ_TILE

    a_ctx = aff_t[:, :m_ctx].reshape(e * n_ctx, ctx_len)
    idx_c, gate_c, _ = _select(a_ctx, cap_c, e * n_ctx, 1, ctx_len, lambda i: (0, 0))
    idx_c = idx_c[:, :cap_c].reshape(e, n_cblk, spb, cap_c) + (jnp.arange(spb) * ctx_len)[None, None, :, None]
    idx_c = jnp.swapaxes(idx_c.reshape(e, n_cblk, cb), 0, 1)
    gate_c = gate_c[:, :cap_c].reshape(e, n_cblk * cb)
    s0_c = jnp.broadcast_to((jnp.arange(spb) * cap_c)[None, :, None], (n_cblk, spb, e))
    cnt_c = jnp.full((n_cblk, spb, e), cap_c, jnp.int32)

    blk0 = m_ctx // lat_len
    idx_l, gate_l, pos_l = _select(aff_t, cb, e, n_lat, lat_len, lambda i: (0, blk0 + i))
    idx_l = idx_l[:, :cb].reshape(n_lat, e, cb)
    gate_l = jnp.swapaxes(gate_l[:, :cb].reshape(n_lat, e, cb), 0, 1).reshape(e, n_lat * cb)
    s0_l = jnp.swapaxes(pos_l[:, ::SEQ_TILE].reshape(n_lat, e, tpb), 1, 2)
    nxt = jnp.concatenate([s0_l[:, 1:], jnp.full((n_lat, 1, e), cb, jnp.int32)], axis=1)
    cnt_l = nxt - s0_l

    idx = jnp.concatenate([idx_c, idx_l], axis=0).reshape((n_cblk + n_lat) * e, 1, cb).astype(jnp.int32)
    gates = jnp.concatenate([gate_c, gate_l], axis=1)
    s0 = jnp.concatenate([s0_c.reshape(-1), s0_l.reshape(-1)]).astype(jnp.int32)
    cnt = jnp.concatenate([cnt_c.reshape(-1), cnt_l.reshape(-1)]).astype(jnp.int32)
    return idx, gates, s0, cnt


def kernel(x_prompt, x_sample, state_rglru, state_ret, c, c_ctx, norm1_g, norm2_g, w_mod, b_mod, w_in, conv_w, conv_b, conv_ln_g, conv_ln_b, rg_conv_w, rg_conv_b, rg_w_a, rg_b_a, rg_w_x, rg_b_x, rg_lambda, ret_decay, ret_gn_g, w_out, router_w, exp_w_gate, exp_w_up, exp_w_down, final_g):
    depth = w_in.shape[0]
    n_ctx, ctx_len, d = x_prompt.shape
    n_lat, lat_len, _ = x_sample.shape
    m_ctx, m_lat = n_ctx * ctx_len, n_lat * lat_len
    assert ctx_len == SEQ_TILE and lat_len % SEQ_TILE == 0 and m_ctx % lat_len == 0
    n_ctx_tiles = m_ctx // SEQ_TILE
    tiles_per_lat = lat_len // SEQ_TILE
    n_seq = n_ctx + n_lat
    n_blocks = (m_ctx + m_lat) // lat_len
    cb = EC_FACTOR * lat_len // N_EXPERTS

    x = (x_prompt.reshape(m_ctx, d), x_sample.reshape(m_lat, d))

    cond = jnp.zeros((MOD_ROWS, d), F32).at[0].set(c_ctx).at[1:1 + n_lat].set(c)
    mod = _modulation(cond, w_mod, b_mod)
    mod3 = mod.reshape(depth * MOD_ROWS * N_MOD, 1, d)

    tables = _rope_tables(lat_len)
    log_g = jax.nn.log_sigmoid(ret_decay.astype(F32))
    log_g3 = jnp.broadcast_to(
        jnp.swapaxes(log_g, 1, 2).reshape(depth * N_RET_HEADS * 2, 1, 1), (depth * N_RET_HEADS * 2, 1, RET_HEAD_DIM))
    zero_ret = jnp.zeros((n_ctx, 2, N_RET_HEADS, RET_HEAD_DIM, RET_HEAD_DIM), F32)

    rg_states, ret_states = [], []
    for l in range(depth):
        proj = _in_proj(x, norm1_g[l], mod3, l, w_in[l].astype(BF16), m_ctx, lat_len, m_ctx + m_lat)

        u = _conv_mixer(proj, conv_w[l], conv_b[l], conv_ln_g[l], conv_ln_b[l], n_ctx_tiles, tiles_per_lat)

        h0 = jnp.concatenate([jnp.zeros((n_ctx, 2, D_RG), F32), state_rglru[:, l]], axis=0)
        wg = [jnp.concatenate([_block_diag(rg_w_a[l, dd]), _block_diag(rg_w_x[l, dd])], axis=1).astype(BF16)
              for dd in range(2)]
        bg = [jnp.concatenate([rg_b_a[l, dd], rg_b_x[l, dd]]) for dd in range(2)]
        hf, st_f = _rglru_dir(proj, None, rg_conv_w[l], rg_conv_b[l], wg[0], bg[0], rg_lambda[l, 0],
                              h0[:, 0].reshape(n_seq, 1, D_RG), False, n_ctx_tiles, tiles_per_lat)
        r_out, st_b = _rglru_dir(proj, hf, rg_conv_w[l], rg_conv_b[l], wg[1], bg[1], rg_lambda[l, 1],
                                 h0[:, 1].reshape(n_seq, 1, D_RG), True, n_ctx_tiles, tiles_per_lat)
        rg_states.append(jnp.stack([st_f[:n_ctx, 0], st_b[:n_ctx, 0]], axis=1))

        lg_l = log_g3[l * N_RET_HEADS * 2:(l + 1) * N_RET_HEADS * 2]
        o_ctx, s_ctx = _retention(proj, lg_l, ret_gn_g[l], zero_ret, None, 0, n_ctx, ctx_len, 0)
        o_lat, _ = _retention(proj, lg_l, ret_gn_g[l], state_ret[:, l], tables, m_ctx // lat_len, n_lat, lat_len, 0)
        ret_states.append(s_ctx)

        x1, h2p, aff_t = _out_proj(u, r_out, (o_ctx, o_lat), w_out[l].astype(BF16), x, norm2_g[l], mod3, l,
                                   router_w[l].T.astype(BF16), m_ctx, lat_len)

        idx, gates, s0, cnt = _route(aff_t, n_ctx, ctx_len, n_lat, lat_len)
        xg = _dispatch(h2p, idx, n_blocks, lat_len, cb)
        hid = _ffn_up(xg, exp_w_gate, exp_w_up, l)
        yp = _pack_y(_ffn_down(hid, exp_w_down, gates, l))
        x = _combine(yp, idx, s0, cnt, x1, mod3, l, final_g, l == depth - 1, m_ctx, lat_len, cb, n_blocks * cb)

    y_prompt = x[0].reshape(n_ctx, ctx_len, d)
    y_sample = x[1].reshape(n_lat, lat_len, d)
    new_state_rglru = jnp.stack(rg_states, axis=1)
    new_state_ret = jnp.stack(ret_states, axis=1)
    return (y_prompt, y_sample, new_state_rglru, new_state_ret)
```
